```python
import math
import jax
import jax.numpy as jnp
from jax import lax
import numpy as np

D_MODEL = 2048
BATCH = 1
SEQ = 8192
DEPTH = 1

CHUNK = 64
EPS = 1e-5

SSD_EXPAND = 2
SSD_D_INNER = SSD_EXPAND * D_MODEL
SSD_HEAD_DIM = 64
SSD_HEADS = SSD_D_INNER // SSD_HEAD_DIM
SSD_GROUPS = 8
SSD_HEADS_PER_GROUP = SSD_HEADS // SSD_GROUPS
SSD_STATE = 128
SSD_CONV = 4
SSD_CONV_DIM = SSD_D_INNER + 2 * SSD_GROUPS * SSD_STATE

MLA_HEADS = 16
MLA_Q_RANK = 512
MLA_KV_RANK = 512
MLA_NOPE = 128
MLA_ROPE = 64
MLA_V = 128
MLA_QK = MLA_NOPE + MLA_ROPE
ROPE_THETA = 10000.0
Q_BLOCK = 128

N_BRANCH = 2

PEER_HEADS = 8
PEER_N_KEYS = 128
PEER_EXPERTS = PEER_N_KEYS * PEER_N_KEYS
PEER_TOPK = 16
PEER_D_KEY = 256
PEER_HALF = PEER_D_KEY // 2
PEER_TOKEN_BLOCK = 128

DEEPNORM_ALPHA = (2 * DEPTH) ** 0.25
DEEPNORM_BETA = (8 * DEPTH) ** -0.25

IN_WIDTHS = (SSD_D_INNER, SSD_CONV_DIM, SSD_HEADS, MLA_Q_RANK, MLA_KV_RANK, MLA_ROPE, N_BRANCH * D_MODEL)
IN_DIM = sum(IN_WIDTHS)
IN_OFFSETS = tuple(int(v) for v in np.cumsum(IN_WIDTHS)[:-1])

kernel_name = 'hybrid_ssd_mla_peer_chunk_causal'


def _normalize(x):
    xf = x.astype(jnp.float32)
    xc = xf - jnp.mean(xf, axis=-1, keepdims=True)
    return xc * lax.rsqrt(jnp.mean(xc * xc, axis=-1, keepdims=True) + EPS)


def layer_norm(x):
    return _normalize(x).astype(x.dtype)


def post_layer_norm(x, gain, bias):
    return (_normalize(x) * gain + bias).astype(x.dtype)


def rms_norm(x, gain):
    xf = x.astype(jnp.float32)
    y = xf * lax.rsqrt(jnp.mean(xf * xf, axis=-1, keepdims=True) + EPS)
    return (y * gain).astype(x.dtype)


def rope_tables(positions):
    inv_freq = ROPE_THETA ** (-jnp.arange(0, MLA_ROPE, 2, dtype=jnp.float32) / MLA_ROPE)
    ang = positions.astype(jnp.float32)[..., None] * inv_freq
    return jnp.cos(ang), jnp.sin(ang)


def apply_rope(x, cos, sin):
    x1, x2 = jnp.split(x.astype(jnp.float32), 2, axis=-1)
    return jnp.concatenate([x1 * cos - x2 * sin, x1 * sin + x2 * cos], axis=-1).astype(x.dtype)


def causal_depthwise_conv(x, w, bias):
    y = lax.conv_general_dilated(x, w.astype(x.dtype)[:, None, :], window_strides=(1,),
                                 padding=((SSD_CONV - 1, 0),),
                                 dimension_numbers=('NWC', 'WIO', 'NWC'),
                                 feature_group_count=x.shape[-1])
    return y + bias


def segsum_exp(a):
    cs = jnp.cumsum(a, axis=-1)
    diff = cs[..., :, None] - cs[..., None, :]
    n = a.shape[-1]
    mask = jnp.tril(jnp.ones((n, n), dtype=bool))
    return jnp.exp(jnp.where(mask, diff, -jnp.inf))


def ssd_chunked_scan(xs, dt, a, bm, cm):
    b, s = xs.shape[:2]
    nc = s // CHUNK
    g, hg, p, n = SSD_GROUPS, SSD_HEADS_PER_GROUP, SSD_HEAD_DIM, SSD_STATE
    xdt = (xs.astype(jnp.float32) * dt[..., None]).reshape(b, nc, CHUNK, g, hg, p)
    ad = (dt * a).reshape(b, nc, CHUNK, g, hg).transpose(0, 3, 4, 1, 2)
    bc = bm.astype(jnp.float32).reshape(b, nc, CHUNK, g, n)
    cc = cm.astype(jnp.float32).reshape(b, nc, CHUNK, g, n)
    a_cs = jnp.cumsum(ad, axis=-1)
    decay = segsum_exp(ad)
    cb = jnp.einsum('bclgn,bcsgn->bgcls', cc, bc)
    y_diag = jnp.einsum('bghcls,bcsghp->bclghp', cb[:, :, None] * decay, xdt)
    to_end = jnp.exp(a_cs[..., -1:] - a_cs).transpose(0, 3, 4, 1, 2)
    states = jnp.einsum('bclgn,bclghp->cbghpn', bc, xdt * to_end[..., None])
    chunk_decay = jnp.exp(a_cs[..., -1]).transpose(3, 0, 1, 2)

    def step(carry, inp):
        dec, st = inp
        return carry * dec[..., None, None] + st, carry

    init = jnp.zeros((b, g, hg, p, n), jnp.float32)
    _, prev = lax.scan(step, init, (chunk_decay, states))
    from_start = jnp.exp(a_cs).transpose(0, 3, 4, 1, 2)
    y_off = jnp.einsum('bclgn,cbghpn->bclghp', cc, prev) * from_start[..., None]
    return (y_diag + y_off).reshape(b, s, g * hg, p)


def ssd_branch(z, xbc, dt_raw, conv_w, conv_b, dt_bias, A_log, D_skip, norm_w, w_out):
    b, s, _ = z.shape
    xbc = jax.nn.silu(causal_depthwise_conv(xbc, conv_w, conv_b))
    xs, bm, cm = jnp.split(xbc, (SSD_D_INNER, SSD_D_INNER + SSD_GROUPS * SSD_STATE), axis=-1)
    xs = xs.reshape(b, s, SSD_HEADS, SSD_HEAD_DIM)
    bm = bm.reshape(b, s, SSD_GROUPS, SSD_STATE)
    cm = cm.reshape(b, s, SSD_GROUPS, SSD_STATE)
    dt = jax.nn.softplus(dt_raw.astype(jnp.float32) + dt_bias.astype(jnp.float32))
    a = -jnp.exp(A_log.astype(jnp.float32))
    y = ssd_chunked_scan(xs, dt, a, bm, cm) + xs.astype(jnp.float32) * D_skip.astype(jnp.float32)[:, None]
    yg = (y.reshape(b, s, SSD_D_INNER) * jax.nn.silu(z.astype(jnp.float32))).reshape(
        b, s, SSD_GROUPS, SSD_D_INNER // SSD_GROUPS)
    yg = yg * lax.rsqrt(jnp.mean(yg * yg, axis=-1, keepdims=True) + EPS)
    y = yg.reshape(b, s, SSD_D_INNER) * norm_w
    return jnp.dot(y.astype(z.dtype), w_out)


def chunk_causal_attention(q, k, v):
    b, s, nh, dqk = q.shape
    scale = dqk ** -0.5
    nblk = s // Q_BLOCK
    key_chunk = jnp.arange(s) // CHUNK
    q_blocks = q.reshape(b, nblk, Q_BLOCK, nh, dqk).transpose(1, 0, 2, 3, 4)

    def one_block(args):
        i, q_blk = args
        scores = jnp.einsum('bqhd,bkhd->bhqk', q_blk, k).astype(jnp.float32) * scale
        q_chunk = (i * Q_BLOCK + jnp.arange(Q_BLOCK)) // CHUNK
        mask = key_chunk[None, :] <= q_chunk[:, None]
        probs = jax.nn.softmax(jnp.where(mask, scores, -jnp.inf), axis=-1).astype(v.dtype)
        return jnp.einsum('bhqk,bkhd->bqhd', probs, v)

    out = lax.map(one_block, (jnp.arange(nblk), q_blocks))
    return out.transpose(1, 0, 2, 3, 4).reshape(b, s, nh, v.shape[-1])


def mla_branch(q_a, kv_a, k_r, cos, sin, q_norm_w, w_uq, kv_norm_w, w_ukv, w_o):
    b, s, _ = q_a.shape
    q = jnp.dot(rms_norm(q_a, q_norm_w), w_uq).reshape(b, s, MLA_HEADS, MLA_QK)
    q_nope, q_pe = jnp.split(q, (MLA_NOPE,), axis=-1)
    q_pe = apply_rope(q_pe, cos[:, :, None, :], sin[:, :, None, :])
    kv = jnp.dot(rms_norm(kv_a, kv_norm_w), w_ukv).reshape(b, s, MLA_HEADS, MLA_NOPE + MLA_V)
    k_nope, v = jnp.split(kv, (MLA_NOPE,), axis=-1)
    k_pe = apply_rope(k_r, cos, sin)
    qf = jnp.concatenate([q_nope, q_pe], axis=-1)
    kf = jnp.concatenate([k_nope, jnp.broadcast_to(k_pe[:, :, None, :], (b, s, MLA_HEADS, MLA_ROPE))], axis=-1)
    o = chunk_causal_attention(qf, kf, v)
    return jnp.dot(o.reshape(b, s, MLA_HEADS * MLA_V), w_o)


def peer_ffn(h, w_q, keys_1, keys_2, u_table, v_table):
    b, s, d = h.shape
    q = jnp.dot(h, w_q).reshape(b, s, PEER_HEADS, 2, PEER_HALF)
    s1 = jnp.einsum('bshd,kd->bshk', q[:, :, :, 0], keys_1)
    s2 = jnp.einsum('bshd,kd->bshk', q[:, :, :, 1], keys_2)
    v1, i1 = lax.top_k(s1, PEER_TOPK)
    v2, i2 = lax.top_k(s2, PEER_TOPK)
    n_cand = PEER_TOPK * PEER_TOPK
    cand_score = (v1[..., :, None] + v2[..., None, :]).reshape(b, s, PEER_HEADS, n_cand)
    cand_index = (i1[..., :, None] * PEER_N_KEYS + i2[..., None, :]).reshape(b, s, PEER_HEADS, n_cand)
    top_score, top_pos = lax.top_k(cand_score, PEER_TOPK)
    expert = jnp.take_along_axis(cand_index, top_pos, axis=-1)
    gate = jax.nn.softmax(top_score.astype(jnp.float32), axis=-1).astype(h.dtype)
    n_sel = PEER_HEADS * PEER_TOPK
    nb = (b * s) // PEER_TOKEN_BLOCK

    def block(args):
        h_blk, e_blk, g_blk = args
        u = jnp.take(u_table, e_blk, axis=0)
        act = jax.nn.gelu(jnp.einsum('td,ted->te', h_blk, u), approximate=False)
        v = jnp.take(v_table, e_blk, axis=0)
        return jnp.einsum('te,ted->td', g_blk * act, v)

    out = lax.map(block, (h.reshape(nb, PEER_TOKEN_BLOCK, d),
                          expert.reshape(nb, PEER_TOKEN_BLOCK, n_sel),
                          gate.reshape(nb, PEER_TOKEN_BLOCK, n_sel)))
    return out.reshape(b, s, d)


def hybrid_layer(x, c, cos, sin, w_ada, b_ada, w_in, b_gate, conv_w, conv_b, dt_bias, A_log, D_skip,
                 ssd_norm_w, w_ssd_out, q_norm_w, w_uq, kv_norm_w, w_ukv, w_mla_out, w_mix_out,
                 ln1_g, ln1_b, peer_w_q, peer_keys_1, peer_keys_2, peer_u, peer_v, ln2_g, ln2_b):
    b, s, d = x.shape
    ada = jnp.dot(jax.nn.silu(c), w_ada) + b_ada
    shift1, scale1, gate1, shift2, scale2, gate2 = jnp.split(ada[:, None, :], 6, axis=-1)

    h = layer_norm(x) * (1 + scale1) + shift1
    z, xbc, dt_raw, q_a, kv_a, k_r, g_logits = jnp.split(jnp.dot(h, w_in), IN_OFFSETS, axis=-1)
    y_ssd = ssd_branch(z, xbc, dt_raw, conv_w, conv_b, dt_bias, A_log, D_skip, ssd_norm_w, w_ssd_out)
    y_mla = mla_branch(q_a, kv_a, k_r, cos, sin, q_norm_w, w_uq, kv_norm_w, w_ukv, w_mla_out)
    g = jax.nn.sigmoid((g_logits + b_gate).astype(jnp.float32)).astype(x.dtype).reshape(b, s, N_BRANCH, d)
    mix = jnp.dot(g[:, :, 0] * y_ssd + g[:, :, 1] * y_mla, w_mix_out)
    x = post_layer_norm(DEEPNORM_ALPHA * x + (1 + gate1) * mix, ln1_g, ln1_b)

    h2 = layer_norm(x) * (1 + scale2) + shift2
    f = peer_ffn(h2, peer_w_q, peer_keys_1, peer_keys_2, peer_u, peer_v)
    return post_layer_norm(DEEPNORM_ALPHA * x + (1 + gate2) * f, ln2_g, ln2_b)


def setup_inputs(seed: int = 0) -> dict:
    key = jax.random.key(seed)
    ks = jax.random.split(key, 32)
    L = DEPTH
    f32 = jnp.float32

    def nrm(k, shape, scale):
        return jax.random.normal(k, shape, f32) * scale

    x = nrm(ks[0], (BATCH, SEQ, D_MODEL), 1.0)
    c = nrm(ks[1], (BATCH, D_MODEL), 1.0)
    offset = jax.random.randint(ks[2], (BATCH, 1), 0, 64, dtype=jnp.int32) * CHUNK
    positions = (offset + jnp.arange(SEQ, dtype=jnp.int32)[None, :]).astype(jnp.int32)
    w_ada = nrm(ks[3], (L, D_MODEL, 6 * D_MODEL), 0.25 * D_MODEL ** -0.5)
    b_ada = nrm(ks[4], (L, 6 * D_MODEL), 0.01)
    w_in = nrm(ks[5], (L, D_MODEL, IN_DIM), D_MODEL ** -0.5)
    b_gate = nrm(ks[6], (L, N_BRANCH * D_MODEL), 0.01)
    conv_w = jax.random.uniform(ks[7], (L, SSD_CONV, SSD_CONV_DIM), f32, -1.0, 1.0) * SSD_CONV ** -0.5
    conv_b = nrm(ks[8], (L, SSD_CONV_DIM), 0.01)
    dt0 = jnp.exp(jax.random.uniform(ks[9], (L, SSD_HEADS), f32, math.log(1e-3), math.log(1e-1)))
    dt_bias = dt0 + jnp.log(-jnp.expm1(-dt0))
    A_log = jnp.log(jax.random.uniform(ks[10], (L, SSD_HEADS), f32, 1.0, 16.0))
    D_skip = 1.0 + nrm(ks[11], (L, SSD_HEADS), 0.01)
    ssd_norm_w = 1.0 + nrm(ks[12], (L, SSD_D_INNER), 0.01)
    w_ssd_out = nrm(ks[13], (L, SSD_D_INNER, D_MODEL), SSD_D_INNER ** -0.5)
    q_norm_w = 1.0 + nrm(ks[14], (L, MLA_Q_RANK), 0.01)
    w_uq = nrm(ks[15], (L, MLA_Q_RANK, MLA_HEADS * MLA_QK), MLA_Q_RANK ** -0.5)
    kv_norm_w = 1.0 + nrm(ks[16], (L, MLA_KV_RANK), 0.01)
    w_ukv = nrm(ks[17], (L, MLA_KV_RANK, MLA_HEADS * (MLA_NOPE + MLA_V)), MLA_KV_RANK ** -0.5)
    w_mla_out = nrm(ks[18], (L, MLA_HEADS * MLA_V, D_MODEL), (MLA_HEADS * MLA_V) ** -0.5)
    w_mix_out = nrm(ks[19], (L, D_MODEL, D_MODEL), DEEPNORM_BETA * D_MODEL ** -0.5)
    ln1_g = 1.0 + nrm(ks[20], (L, D_MODEL), 0.01)
    ln1_b = nrm(ks[21], (L, D_MODEL), 0.01)
    peer_w_q = nrm(ks[22], (L, D_MODEL, PEER_HEADS * PEER_D_KEY), D_MODEL ** -0.5)
    peer_keys_1 = nrm(ks[23], (L, PEER_N_KEYS, PEER_HALF), PEER_HALF ** -0.5)
    peer_keys_2 = nrm(ks[24], (L, PEER_N_KEYS, PEER_HALF), PEER_HALF ** -0.5)
    peer_u = nrm(ks[25], (L, PEER_EXPERTS, D_MODEL), D_MODEL ** -0.5)
    peer_v = nrm(ks[26], (L, PEER_EXPERTS, D_MODEL), DEEPNORM_BETA)
    ln2_g = 1.0 + nrm(ks[27], (L, D_MODEL), 0.01)
    ln2_b = nrm(ks[28], (L, D_MODEL), 0.01)
    return {'x': x, 'c': c, 'positions': positions, 'w_ada': w_ada, 'b_ada': b_ada, 'w_in': w_in,
            'b_gate': b_gate, 'conv_w': conv_w, 'conv_b': conv_b, 'dt_bias': dt_bias, 'A_log': A_log,
            'D_skip': D_skip, 'ssd_norm_w': ssd_norm_w, 'w_ssd_out': w_ssd_out, 'q_norm_w': q_norm_w,
            'w_uq': w_uq, 'kv_norm_w': kv_norm_w, 'w_ukv': w_ukv, 'w_mla_out': w_mla_out,
            'w_mix_out': w_mix_out, 'ln1_g': ln1_g, 'ln1_b': ln1_b, 'peer_w_q': peer_w_q,
            'peer_keys_1': peer_keys_1, 'peer_keys_2': peer_keys_2, 'peer_u': peer_u, 'peer_v': peer_v,
            'ln2_g': ln2_g, 'ln2_b': ln2_b}


def reference(x, c, positions, w_ada, b_ada, w_in, b_gate, conv_w, conv_b, dt_bias, A_log, D_skip,
              ssd_norm_w, w_ssd_out, q_norm_w, w_uq, kv_norm_w, w_ukv, w_mla_out, w_mix_out,
              ln1_g, ln1_b, peer_w_q, peer_keys_1, peer_keys_2, peer_u, peer_v, ln2_g, ln2_b):
    cos, sin = rope_tables(positions)
    for l in range(DEPTH):
        x = hybrid_layer(x, c, cos, sin, w_ada[l], b_ada[l], w_in[l], b_gate[l], conv_w[l], conv_b[l],
                         dt_bias[l], A_log[l], D_skip[l], ssd_norm_w[l], w_ssd_out[l], q_norm_w[l],
                         w_uq[l], kv_norm_w[l], w_ukv[l], w_mla_out[l], w_mix_out[l], ln1_g[l], ln1_b[l],
                         peer_w_q[l], peer_keys_1[l], peer_keys_2[l], peer_u[l], peer_v[l],
                         ln2_g[l], ln2_b[l])
    return x
```

```python
import functools

import numpy as np
import jax
import jax.numpy as jnp
from jax import lax
from jax.experimental import pallas as pl
from jax.experimental.pallas import tpu as pltpu

F32 = jnp.float32
BF16 = jnp.bfloat16
HI = lax.Precision.HIGHEST
NT_DIMS = (((1,), (1,)), ((), ()))

CHUNK = 64
EPS = 1e-5
SSD_HEAD_DIM = 64
SSD_GROUPS = 8
SSD_STATE = 128
SSD_CONV = 4
MLA_HEADS = 16
MLA_NOPE = 128
MLA_ROPE = 64
MLA_V = 128
MLA_QK = MLA_NOPE + MLA_ROPE
MLA_HEAD_PAD = 256
ROPE_THETA = 10000.0
N_BRANCH = 2
PEER_HEADS = 8
PEER_N_KEYS = 128
PEER_TOPK = 16
PEER_HALF = 128
LANES = 128
VMEM_LIMIT = 56 * 1024 * 1024


def _cp(*sem, vmem=VMEM_LIMIT):
    return pltpu.CompilerParams(dimension_semantics=sem, vmem_limit_bytes=vmem)


def _silu(v):
    return v * jax.nn.sigmoid(v)


def _normalize(v):
    mu = jnp.mean(v, axis=-1, keepdims=True)
    vc = v - mu
    var = jnp.mean(vc * vc, axis=-1, keepdims=True)
    return vc * lax.rsqrt(var + EPS)


def _ada_body(c_ref, w_ref, b_ref, o_ref, *, rows):
    d = c_ref.shape[0]
    tn = o_ref.shape[1]

    def step(k, acc):
        r0 = pl.multiple_of(k * rows, rows)
        cc = c_ref[pl.ds(r0, rows), :]
        return acc + w_ref[pl.ds(r0, rows), :] * _silu(cc)

    acc = lax.fori_loop(0, d // rows, step, jnp.zeros((rows, tn), F32))
    o_ref[...] = jnp.sum(acc, axis=0, keepdims=True) + b_ref[...]


def _ada(c, w_ada, b_ada, tn=512, rows=64):
    d, n = w_ada.shape
    return pl.pallas_call(
        functools.partial(_ada_body, rows=rows),
        grid=(n // tn,),
        in_specs=[pl.BlockSpec((d, 1), lambda j: (0, 0)),
                  pl.BlockSpec((d, tn), lambda j: (0, j)),
                  pl.BlockSpec((1, tn), lambda j: (0, j))],
        out_specs=pl.BlockSpec((1, tn), lambda j: (0, j)),
        out_shape=jax.ShapeDtypeStruct((1, n), F32),
        compiler_params=_cp("arbitrary"),
        name="ada",
    )(c.reshape(d, 1), w_ada, b_ada.reshape(1, n))


def _ln_mod_body(x_ref, sc_ref, sh_ref, o_ref):
    y = _normalize(x_ref[...])
    o_ref[...] = (y * (1.0 + sc_ref[...]) + sh_ref[...]).astype(o_ref.dtype)


def _ln_mod(x, scale, shift, tm=256):
    s, d = x.shape
    vec = pl.BlockSpec((1, d), lambda i: (0, 0))
    return pl.pallas_call(
        _ln_mod_body,
        grid=(s // tm,),
        in_specs=[pl.BlockSpec((tm, d), lambda i: (i, 0)), vec, vec],
        out_specs=pl.BlockSpec((tm, d), lambda i: (i, 0)),
        out_shape=jax.ShapeDtypeStruct((s, d), BF16),
        compiler_params=_cp("arbitrary"),
        name="ln_mod",
    )(x, scale, shift)


def _mm_body(a_ref, w_ref, o_ref, wbf_ref):
    @pl.when(pl.program_id(1) == 0)
    def _():
        wbf_ref[...] = w_ref[...].astype(BF16)

    o_ref[...] = jnp.dot(a_ref[...], wbf_ref[...], preferred_element_type=F32).astype(o_ref.dtype)


def _mm(a, w, *, col0=0, n=None, out_dtype=BF16, tm=1024, tn=1024, name="mm"):
    m, k = a.shape
    n = w.shape[1] - col0 if n is None else n
    tn = min(tn, n)
    tm = min(tm, m)
    assert n % tn == 0 and col0 % tn == 0 and m % tm == 0
    cb = col0 // tn
    return pl.pallas_call(
        _mm_body,
        grid=(n // tn, m // tm),
        in_specs=[pl.BlockSpec((tm, k), lambda j, i: (i, 0)),
                  pl.BlockSpec((k, tn), lambda j, i: (0, j + cb))],
        out_specs=pl.BlockSpec((tm, tn), lambda j, i: (i, j)),
        out_shape=jax.ShapeDtypeStruct((m, n), out_dtype),
        scratch_shapes=[pltpu.VMEM((k, tn), BF16)],
        compiler_params=_cp("arbitrary", "arbitrary"),
        name=name,
    )(a, w)


def _split3(v):
    hi = v.astype(BF16)
    r = v - hi.astype(F32)
    mid = r.astype(BF16)
    lo = (r - mid.astype(F32)).astype(BF16)
    return hi, mid, lo


def _expand(v, e):
    out = None
    for p in _split3(v):
        t = jnp.dot(p, e, preferred_element_type=F32)
        out = t if out is None else out + t
    return out


def _ssd_body(z_ref, xs_ref, bc_ref, dt_ref, cw_ref, cb_ref, dtb_ref, alog_ref, dexp_ref, nw_ref,
              e_ref, e128_ref, o_ref, xbuf, state, csx, fsx, wex, cst, dtt, xpost):
    L = CHUNK
    P = SSD_HEAD_DIM
    NS = SSD_STATE
    G = SSD_GROUPS
    d_inner = z_ref.shape[1]
    gw = d_inner // G
    hpg = gw // P
    nh = d_inner // P
    conv_dim = d_inner + 2 * G * NS
    halo = 8

    @pl.when(pl.program_id(0) == 0)
    def _():
        xbuf[0:halo, :] = jnp.zeros((halo, conv_dim), F32)
        state[...] = jnp.zeros(state.shape, F32)

    xbuf[halo:halo + L, 0:d_inner] = xs_ref[...].astype(F32)
    xbuf[halo:halo + L, d_inner:conv_dim] = bc_ref[...].astype(F32)
    slab = 512

    def conv_slab(j, _):
        c0 = pl.multiple_of(j * slab, slab)
        xe = xbuf[:, pl.ds(c0, slab)]
        w = cw_ref[:, pl.ds(c0, slab)]
        acc = cb_ref[:, pl.ds(c0, slab)] + xe[halo:halo + L] * w[SSD_CONV - 1:SSD_CONV]
        for sft in range(1, SSD_CONV):
            acc = acc + pltpu.roll(xe, sft, axis=0)[halo:halo + L] * w[SSD_CONV - 1 - sft:SSD_CONV - sft]
        xpost[:, pl.ds(c0, slab)] = _silu(acc)
        return 0

    lax.fori_loop(0, conv_dim // slab, conv_slab, 0)
    xbuf[0:halo, :] = xbuf[L:L + halo, :]

    ri = lax.broadcasted_iota(jnp.int32, (L, L), 0)
    ci = lax.broadcasted_iota(jnp.int32, (L, L), 1)
    tril = ri >= ci
    dt = jax.nn.softplus(dt_ref[...][:, 0:nh] + dtb_ref[...])
    ad = dt * (-jnp.exp(alog_ref[...]))
    cs = jnp.dot(tril.astype(F32), ad, precision=HI, preferred_element_type=F32)
    tot = cs[L - 1:L, :]
    wend = dt * jnp.exp(tot - cs)
    eye = (lax.broadcasted_iota(jnp.int32, (nh, nh), 0) == lax.broadcasted_iota(jnp.int32, (nh, nh), 1)).astype(F32)
    cst[...] = lax.dot_general(eye, cs, NT_DIMS, precision=HI, preferred_element_type=F32)
    dtt[...] = lax.dot_general(eye, dt, NT_DIMS, precision=HI, preferred_element_type=F32)
    csx[...] = _expand(cs, e128_ref[...])
    both = _expand(jnp.concatenate([cs, wend], axis=0), e_ref[...])
    fsx[...] = jnp.exp(both[0:L])
    wex[...] = both[L:2 * L]
    lane = lax.broadcasted_iota(jnp.int32, (L, LANES), 1)
    r128 = lax.broadcasted_iota(jnp.int32, (NS, NS), 0)
    c128 = lax.broadcasted_iota(jnp.int32, (NS, NS), 1)
    eye_n = (r128 == c128).astype(BF16)

    def group(g, _):
        o_g = pl.multiple_of(g * gw, gw)
        o_n = pl.multiple_of(g * NS, NS)
        bg = xpost[:, pl.ds(d_inner + o_n, NS)].astype(BF16)
        cg = xpost[:, pl.ds(d_inner + G * NS + o_n, NS)].astype(BF16)
        xg = xpost[:, pl.ds(o_g, gw)]
        xgb = xg.astype(BF16)
        cb = lax.dot_general(cg, bg, NT_DIMS, preferred_element_type=F32)
        st = state[g]
        fs_g = fsx[:, pl.ds(o_g, gw)]
        yoff = jnp.dot(cg, st.astype(BF16), preferred_element_type=F32) * fs_g
        parts = []
        for pair in range(gw // LANES):
            xpair = xgb[:, pair * LANES:(pair + 1) * LANES]
            zs = []
            for q in range(LANES // P):
                hh = pair * (LANES // P) + q
                h = g * hpg + hh
                row = csx[:, pl.ds(pl.multiple_of(h * LANES, LANES), LANES)][:, 0:L]
                col = cst[pl.ds(h, 1), :]
                dcol = dtt[pl.ds(h, 1), :]
                decay = jnp.exp(jnp.where(tril, row - col, -jnp.inf)) * dcol
                mh = (cb * decay).astype(BF16)
                zs.append(jnp.dot(mh, xpair, preferred_element_type=F32))
            acc = zs[-1]
            for q in range(LANES // P - 2, -1, -1):
                acc = jnp.where(lane < (q + 1) * P, zs[q], acc)
            parts.append(acc)
        ydiag = jnp.concatenate(parts, axis=1)
        xw = (xg * wex[:, pl.ds(o_g, gw)]).astype(BF16)
        bgt = lax.dot_general(eye_n, bg, NT_DIMS, preferred_element_type=F32).astype(BF16)
        new = jnp.dot(bgt, xw, preferred_element_type=F32)
        state[g] = st * fs_g[L - 1:L, :] + new
        y = ydiag + yoff + xg * dexp_ref[:, pl.ds(o_g, gw)]
        yg = y * _silu(z_ref[:, pl.ds(o_g, gw)].astype(F32))
        ms = jnp.mean(yg * yg, axis=-1, keepdims=True)
        o_ref[:, pl.ds(o_g, gw)] = (yg * lax.rsqrt(ms + EPS) * nw_ref[:, pl.ds(o_g, gw)]).astype(o_ref.dtype)
        return 0

    lax.fori_loop(0, G, group, 0)


def _ssd(proj, small, conv_w, conv_b, dt_bias, a_log, d_skip, norm_w):
    s = proj.shape[0]
    d_inner = norm_w.shape[0]
    nh = d_inner // SSD_HEAD_DIM
    gn = SSD_GROUPS * SSD_STATE
    conv_dim = d_inner + 2 * gn
    gw = d_inner // SSD_GROUPS
    L = CHUNK
    assert d_inner % gn == 0 and nh <= LANES
    dt_blk = (small.shape[1] - LANES) // LANES
    e =jnp.asarray(np.repeat(np.eye(nh, dtype=np.float32), SSD_HEAD_DIM, axis=1), BF16)
    e128 = jnp.asarray(np.repeat(np.eye(nh, dtype=np.float32), LANES, axis=1), BF16)
    dexp = jnp.repeat(d_skip, SSD_HEAD_DIM).reshape(1, d_inner)
    full = lambda shape: pl.BlockSpec(shape, lambda c: (0,) * len(shape))
    return pl.pallas_call(
        _ssd_body,
        grid=(s // L,),
        in_specs=[pl.BlockSpec((L, d_inner), lambda c: (c, 0)),
                  pl.BlockSpec((L, d_inner), lambda c: (c, 1)),
                  pl.BlockSpec((L, 2 * gn), lambda c: (c, d_inner // gn)),
                  pl.BlockSpec((L, LANES), lambda c: (c, dt_blk)),
                  full((SSD_CONV, conv_dim)), full((1, conv_dim)), full((1, nh)), full((1, nh)),
                  full((1, d_inner)), full((1, d_inner)), full((nh, d_inner)), full((nh, nh * LANES))],
        out_specs=pl.BlockSpec((L, d_inner), lambda c: (c, 0)),
        out_shape=jax.ShapeDtypeStruct((s, d_inner), BF16),
        scratch_shapes=[pltpu.VMEM((L + 8, conv_dim), F32),
                        pltpu.VMEM((SSD_GROUPS, SSD_STATE, gw), F32),
                        pltpu.VMEM((L, nh * LANES), F32),
                        pltpu.VMEM((L, d_inner), F32),
                        pltpu.VMEM((L, d_inner), F32),
                        pltpu.VMEM((nh, L), F32), pltpu.VMEM((nh, L), F32),
                        pltpu.VMEM((L, conv_dim), F32)],
        compiler_params=_cp("arbitrary"),
        name="ssd",
    )(proj, proj, proj, small, conv_w, conv_b.reshape(1, conv_dim), dt_bias.reshape(1, nh), a_log.reshape(1, nh),
      dexp, norm_w.reshape(1, d_inner), e, e128)


def _rope_body(pos_ref, inv_ref, c_ref, sa_ref, sb_ref):
    half = MLA_ROPE // 2
    ang = pos_ref[...].astype(F32) * inv_ref[...]
    lane = lax.broadcasted_iota(jnp.int32, ang.shape, 1)
    cos = jnp.cos(ang)
    sin = jnp.sin(ang)
    c_ref[...] = jnp.where(lane < MLA_ROPE, cos, 0.0)
    sa_ref[...] = jnp.where(lane < half, -sin, 0.0)
    sb_ref[...] = jnp.where((lane >= half) & (lane < MLA_ROPE), sin, 0.0)


def _rope_tables(positions, tm=1024):
    s = positions.shape[0]
    inv = ROPE_THETA ** (-jnp.arange(0, MLA_ROPE, 2, dtype=F32) / MLA_ROPE)
    inv = jnp.concatenate([inv, inv, jnp.zeros((LANES - MLA_ROPE,), F32)]).reshape(1, LANES)
    posb = jnp.broadcast_to(positions.reshape(s, 1), (s, LANES))
    blk = pl.BlockSpec((tm, LANES), lambda i: (i, 0))
    return pl.pallas_call(
        _rope_body,
        grid=(s // tm,),
        in_specs=[blk, pl.BlockSpec((1, LANES), lambda i: (0, 0))],
        out_specs=[blk, blk, blk],
        out_shape=[jax.ShapeDtypeStruct((s, LANES), F32)] * 3,
        compiler_params=_cp("arbitrary"),
        name="rope",
    )(posb, inv)


def _rope(v, c, sa, sb):
    half = MLA_ROPE // 2
    return v * c + pltpu.roll(v, LANES - half, axis=1) * sa + pltpu.roll(v, half, axis=1) * sb


def _rms(v, gain):
    return v * lax.rsqrt(jnp.mean(v * v, axis=-1, keepdims=True) + EPS) * gain


def _qproj_body(a_ref, g_ref, w_ref, c_ref, sa_ref, sb_ref, o_ref):
    scale = MLA_QK ** -0.5
    qn = _rms(a_ref[...], g_ref[...]).astype(BF16)
    c, sa, sb = c_ref[...], sa_ref[...], sb_ref[...]
    for h in range(MLA_HEADS):
        lo = h * MLA_HEAD_PAD
        q = jnp.dot(qn, w_ref[:, lo:lo + MLA_HEAD_PAD], preferred_element_type=F32)
        o_ref[:, lo:lo + LANES] = (q[:, 0:LANES] * scale).astype(o_ref.dtype)
        o_ref[:, lo + LANES:lo + 2 * LANES] = (_rope(q[:, LANES:2 * LANES], c, sa, sb) * scale).astype(o_ref.dtype)


def _kvproj_body(a_ref, kr_ref, g_ref, wk_ref, wv_ref, c_ref, sa_ref, sb_ref, k_ref, v_ref):
    kn = _rms(a_ref[...], g_ref[...]).astype(BF16)
    kr = pltpu.roll(kr_ref[...], LANES - MLA_ROPE, axis=1)
    kpe = _rope(kr, c_ref[...], sa_ref[...], sb_ref[...]).astype(k_ref.dtype)
    v_ref[...] = jnp.dot(kn, wv_ref[...], preferred_element_type=F32).astype(v_ref.dtype)
    for h in range(MLA_HEADS):
        lo = h * MLA_HEAD_PAD
        k_ref[:, lo:lo + LANES] = jnp.dot(kn, wk_ref[:, h * MLA_NOPE:(h + 1) * MLA_NOPE],
                                          preferred_element_type=F32).astype(k_ref.dtype)
        k_ref[:, lo + LANES:lo + 2 * LANES] = kpe


def _mla_proj(small, q_gain, wq, kv_gain, wk, wv, c, sa, sb, tm=256):
    s = small.shape[0]
    rank = q_gain.shape[0]
    assert kv_gain.shape[0] == rank and rank % LANES == 0
    tab = pl.BlockSpec((tm, LANES), lambda i: (i, 0))
    full = lambda a: pl.BlockSpec(a.shape, lambda i: (0, 0))
    qg, kg = q_gain.reshape(1, rank), kv_gain.reshape(1, rank)
    qf = pl.pallas_call(
        _qproj_body,
        grid=(s // tm,),
        in_specs=[pl.BlockSpec((tm, rank), lambda i: (i, 0)), full(qg), full(wq), tab, tab, tab],
        out_specs=pl.BlockSpec((tm, MLA_HEADS * MLA_HEAD_PAD), lambda i: (i, 0)),
        out_shape=jax.ShapeDtypeStruct((s, MLA_HEADS * MLA_HEAD_PAD), BF16),
        compiler_params=_cp("arbitrary"),
        name="qproj",
    )(small, qg, wq, c, sa, sb)
    kf, v = pl.pallas_call(
        _kvproj_body,
        grid=(s // tm,),
        in_specs=[pl.BlockSpec((tm, rank), lambda i: (i, 1)),
                  pl.BlockSpec((tm, LANES), lambda i: (i, 2 * rank // LANES)),
                  full(kg), full(wk), full(wv), tab, tab, tab],
        out_specs=[pl.BlockSpec((tm, MLA_HEADS * MLA_HEAD_PAD), lambda i: (i, 0)),
                   pl.BlockSpec((tm, MLA_HEADS * MLA_V), lambda i: (i, 0))],
        out_shape=[jax.ShapeDtypeStruct((s, MLA_HEADS * MLA_HEAD_PAD), BF16),
                   jax.ShapeDtypeStruct((s, MLA_HEADS * MLA_V), BF16)],
        compiler_params=_cp("arbitrary"),
        name="kvproj",
    )(small, small, kg, wk, wv, c, sa, sb)
    return qf, kf, v


def _attn_body(q_ref, k_ref, v_ref, o_ref, m_scr, l_scr, acc_scr, *, bq, bk):
    i = pl.program_id(1)
    q = q_ref[...]
    m_scr[...] = jnp.full(m_scr.shape, -jnp.inf, F32)
    l_scr[...] = jnp.zeros(l_scr.shape, F32)
    acc_scr[...] = jnp.zeros(acc_scr.shape, F32)

    def block(j, masked):
        k0 = pl.multiple_of(j * bk, bk)
        s = lax.dot_general(q, k_ref[pl.ds(k0, bk), :], NT_DIMS, preferred_element_type=F32)
        if masked:
            qc = (i * bq + lax.broadcasted_iota(jnp.int32, (bq, bk), 0)) // CHUNK
            kc = (k0 + lax.broadcasted_iota(jnp.int32, (bq, bk), 1)) // CHUNK
            s = jnp.where(kc <= qc, s, -jnp.inf)
        m_prev = m_scr[...]
        m_next = jnp.maximum(m_prev, jnp.max(s, axis=1, keepdims=True))
        p = jnp.exp(s - pltpu.repeat(m_next, bk // LANES, axis=1))
        alpha = jnp.exp(m_prev - m_next)
        l_scr[...] = alpha * l_scr[...] + jnp.sum(p, axis=1, keepdims=True)
        acc_scr[...] = alpha * acc_scr[...] + jnp.dot(p.astype(BF16), v_ref[pl.ds(k0, bk), :],
                                                      preferred_element_type=F32)
        m_scr[...] = m_next

    nfull = (i * bq) // bk

    def body(j, _):
        block(j, False)
        return 0

    lax.fori_loop(0, nfull, body, 0)
    for d in range(bq // bk):
        block(nfull + d, True)
    o_ref[...] = (acc_scr[...] / l_scr[...]).astype(o_ref.dtype)


def _attn(qf, kf, v, bq=512, bk=512):
    s = qf.shape[0]
    assert bq % bk == 0 and bk % CHUNK == 0 and MLA_V == LANES
    return pl.pallas_call(
        functools.partial(_attn_body, bq=bq, bk=bk),
        grid=(MLA_HEADS, s // bq),
        in_specs=[pl.BlockSpec((bq, MLA_HEAD_PAD), lambda h, i: (i, h)),
                  pl.BlockSpec((s, MLA_HEAD_PAD), lambda h, i: (0, h)),
                  pl.BlockSpec((s, MLA_V), lambda h, i: (0, h))],
        out_specs=pl.BlockSpec((bq, MLA_V), lambda h, i: (i, h)),
        out_shape=jax.ShapeDtypeStruct((s, MLA_HEADS * MLA_V), BF16),
        scratch_shapes=[pltpu.VMEM((bq, LANES), F32), pltpu.VMEM((bq, LANES), F32), pltpu.VMEM((bq, MLA_V), F32)],
        compiler_params=_cp("arbitrary", "arbitrary"),
        name="attn",
    )(qf, kf, v)


def _merge_body(ys_ref, ym_ref, ws_ref, wm_ref, g0_ref, g1_ref, b0_ref, b1_ref, o_ref, wsb, wmb):
    @pl.when(pl.program_id(1) == 0)
    def _():
        wsb[...] = ws_ref[...].astype(BF16)
        wmb[...] = wm_ref[...].astype(BF16)

    y_ssd = jnp.dot(ys_ref[...], wsb[...], preferred_element_type=F32)
    y_mla = jnp.dot(ym_ref[...], wmb[...], preferred_element_type=F32)
    g0 = jax.nn.sigmoid(g0_ref[...].astype(F32) + b0_ref[...])
    g1 = jax.nn.sigmoid(g1_ref[...].astype(F32) + b1_ref[...])
    o_ref[...] = (g0 * y_ssd + g1 * y_mla).astype(o_ref.dtype)


def _merge(yn, o, w_ssd_out, w_mla_out, glog, b_gate, tm=512, tn=512):
    s, d = yn.shape[0], w_ssd_out.shape[1]
    ks, km = w_ssd_out.shape[0], w_mla_out.shape[0]
    nb = d // tn
    bg = b_gate.reshape(1, N_BRANCH * d)
    return pl.pallas_call(
        _merge_body,
        grid=(nb, s // tm),
        in_specs=[pl.BlockSpec((tm, ks), lambda j, i: (i, 0)), pl.BlockSpec((tm, km), lambda j, i: (i, 0)),
                  pl.BlockSpec((ks, tn), lambda j, i: (0, j)), pl.BlockSpec((km, tn), lambda j, i: (0, j)),
                  pl.BlockSpec((tm, tn), lambda j, i: (i, j)), pl.BlockSpec((tm, tn), lambda j, i: (i, j + nb)),
                  pl.BlockSpec((1, tn), lambda j, i: (0, j)), pl.BlockSpec((1, tn), lambda j, i: (0, j + nb))],
        out_specs=pl.BlockSpec((tm, tn), lambda j, i: (i, j)),
        out_shape=jax.ShapeDtypeStruct((s, d), BF16),
        scratch_shapes=[pltpu.VMEM((ks, tn), BF16), pltpu.VMEM((km, tn), BF16)],
        compiler_params=_cp("arbitrary", "arbitrary"),
        name="merge",
    )(yn, o, w_ssd_out, w_mla_out, glog, glog, bg, bg)


def _post_ln_mid_body(x_ref, f_ref, gate_ref, g_ref, b_ref, sc_ref, sh_ref, x_out, h_out, *, alpha):
    y = _normalize(alpha * x_ref[...] + (1.0 + gate_ref[...]) * f_ref[...]) * g_ref[...] + b_ref[...]
    x_out[...] = y
    h_out[...] = (_normalize(y) * (1.0 + sc_ref[...]) + sh_ref[...]).astype(h_out.dtype)


def _post_ln_mid(x, f, gate, g, b, scale, shift, alpha, tm=256):
    s, d = x.shape
    vec = pl.BlockSpec((1, d), lambda i: (0, 0))
    row = pl.BlockSpec((tm, d), lambda i: (i, 0))
    return pl.pallas_call(
        functools.partial(_post_ln_mid_body, alpha=alpha),
        grid=(s // tm,),
        in_specs=[row, row, vec, vec, vec, vec, vec],
        out_specs=[row, row],
        out_shape=[jax.ShapeDtypeStruct((s, d), F32), jax.ShapeDtypeStruct((s, d), BF16)],
        compiler_params=_cp("arbitrary"),
        name="post_ln_mid",
    )(x, f, gate, g.reshape(1, d), b.reshape(1, d), scale, shift)


def _post_ln_out_body(x_ref, ft_ref, gate_ref, g_ref, b_ref, x_out, *, alpha):
    f = ft_ref[...].T
    x_out[...] = _normalize(alpha * x_ref[...] + (1.0 + gate_ref[...]) * f) * g_ref[...] + b_ref[...]


def _post_ln_out(x, ft, gate, g, b, alpha, tm=256):
    s, d = x.shape
    vec = pl.BlockSpec((1, d), lambda i: (0, 0))
    row = pl.BlockSpec((tm, d), lambda i: (i, 0))
    return pl.pallas_call(
        functools.partial(_post_ln_out_body, alpha=alpha),
        grid=(s // tm,),
        in_specs=[row, pl.BlockSpec((d, tm), lambda i: (0, i)), vec, vec, vec],
        out_specs=row,
        out_shape=jax.ShapeDtypeStruct((s, d), F32),
        compiler_params=_cp("arbitrary"),
        name="post_ln_out",
    )(x, ft, gate, g.reshape(1, d), b.reshape(1, d))


def _topk_desc(v, k):
    vals = []
    cur = v
    for _ in range(k):
        m = jnp.max(cur, axis=0, keepdims=True)
        vals.append(m)
        cur = jnp.where(cur == m, -jnp.inf, cur)
    return jnp.concatenate(vals, axis=0)


def _peer_thr_body(q_ref, k1_ref, k2_ref, p1_ref, pth_ref, p2_ref):
    K = PEER_TOPK
    q = q_ref[...]
    s1 = lax.dot_general(k1_ref[...].astype(BF16), q[:, 0:PEER_HALF], NT_DIMS, preferred_element_type=F32)
    s2 = lax.dot_general(k2_ref[...].astype(BF16), q[:, PEER_HALF:2 * PEER_HALF], NT_DIMS,
                         preferred_element_type=F32)
    v1 = _topk_desc(s1, K)
    v2 = _topk_desc(s2, K)
    cand = jnp.concatenate([v1[a:a + 1] + v2 for a in range(K)], axis=0)
    tau = _topk_desc(cand, K)[K - 1:K]
    m1, m2 = v1[0:1], v2[0:1]
    e1 = jnp.exp(v1 - m1)
    e2 = jnp.exp(v2 - m2)
    ecand = jnp.concatenate([e1[a:a + 1] * e2 for a in range(K)], axis=0)
    z = jnp.sum(jnp.where(cand >= tau, ecand, 0.0), axis=0, keepdims=True)
    inv_z = 1.0 / z
    thr2 = jnp.full(s1.shape, jnp.inf, F32)
    for b in range(K):
        thr2 = jnp.where(s1 + v2[b:b + 1] >= tau, v2[b:b + 1], thr2)
    p1_ref[0] = jnp.exp(s1 - m1)
    p2_ref[0] = jnp.exp(s2 - m2) * inv_z
    pth_ref[0] = jnp.exp(thr2 - m2) * inv_z


def _peer_thr(q, keys_1, keys_2, tm=256):
    s = q.shape[0]
    nk = keys_1.shape[0]
    blk = pl.BlockSpec((1, nk, tm), lambda i, h: (h, 0, i))
    kspec = pl.BlockSpec(keys_1.shape, lambda i, h: (0, 0))
    out = jax.ShapeDtypeStruct((PEER_HEADS, nk, s), F32)
    return pl.pallas_call(
        _peer_thr_body,
        grid=(s // tm, PEER_HEADS),
        in_specs=[pl.BlockSpec((tm, 2 * PEER_HALF), lambda i, h: (i, h)), kspec, kspec],
        out_specs=[blk, blk, blk],
        out_shape=[out, out, out],
        compiler_params=_cp("arbitrary", "arbitrary"),
        name="peer_thr",
    )(q, keys_1, keys_2)


def _gelu(v):
    return 0.5 * v * (1.0 + lax.erf(v * (2.0 ** -0.5)))


def _peer_main_body(h_ref, u_ref, vt_ref, p1_ref, pth_ref, p2_ref, o_ref, at_scr, g_scr):
    e = pl.program_id(1)
    te, tm = g_scr.shape
    nk = PEER_N_KEYS
    nblk = te // nk

    @pl.when(e == 0)
    def _():
        o_ref[...] = jnp.zeros(o_ref.shape, F32)

    at_scr[...] = lax.dot_general(u_ref[...], h_ref[...], NT_DIMS, preferred_element_type=F32)

    def lane_chunk(lc, _):
        cols = pl.ds(pl.multiple_of(lc * LANES, LANES), LANES)
        for ii in range(nblk):
            rows = slice(ii * nk, (ii + 1) * nk)
            w = jnp.zeros((nk, LANES), F32)
            for h in range(PEER_HEADS):
                r0 = pl.multiple_of(h * nk + e * nblk, nblk)
                th = pth_ref[pl.ds(r0, nblk), cols][ii:ii + 1]
                p1 = p1_ref[pl.ds(r0, nblk), cols][ii:ii + 1]
                p2 = p2_ref[h, :, cols]
                w = w + jnp.where(p2 >= th, p2, 0.0) * p1
            g_scr[rows, cols] = (_gelu(at_scr[rows, cols]) * w).astype(g_scr.dtype)
        return 0

    lax.fori_loop(0, tm // LANES, lane_chunk, 0)
    o_ref[...] += jnp.dot(vt_ref[...], g_scr[...], preferred_element_type=F32)


def _peer_main(h2, u_bf, vt_bf, p1, pth, p2, tm=512, te=1024):
    s, d = h2.shape
    ne = u_bf.shape[0]
    nk = PEER_N_KEYS
    assert te // nk == 8
    pblk = pl.BlockSpec((PEER_HEADS, nk, tm), lambda i, e: (0, 0, i))
    rblk = pl.BlockSpec((PEER_HEADS * nk, tm), lambda i, e: (0, i))
    return pl.pallas_call(
        _peer_main_body,
        grid=(s // tm, ne // te),
        in_specs=[pl.BlockSpec((tm, d), lambda i, e: (i, 0)),
                  pl.BlockSpec((te, d), lambda i, e: (e, 0)),
                  pl.BlockSpec((d, te), lambda i, e: (0, e)),
                  rblk, rblk, pblk],
        out_specs=pl.BlockSpec((d, tm), lambda i, e: (0, i)),
        out_shape=jax.ShapeDtypeStruct((d, s), F32),
        scratch_shapes=[pltpu.VMEM((te, tm), F32), pltpu.VMEM((te, tm), BF16)],
        compiler_params=_cp("arbitrary", "arbitrary"),
        name="peer_main",
    )(h2, u_bf, vt_bf, p1.reshape(PEER_HEADS * nk, s), pth.reshape(PEER_HEADS * nk, s), p2)


def _layer(depth, x, c, cs_tabs, w_ada, b_ada, w_in, b_gate, conv_w, conv_b, dt_bias, a_log, d_skip, ssd_norm_w, w_ssd_out,
           q_norm_w, w_uq, kv_norm_w, w_ukv, w_mla_out, w_mix_out, ln1_g, ln1_b, peer_w_q, keys_1, keys_2,
           peer_u, peer_v, ln2_g, ln2_b):
    s, d = x.shape
    d_inner = ssd_norm_w.shape[0]
    gn = SSD_GROUPS * SSD_STATE
    nh = d_inner // SSD_HEAD_DIM
    rq, rkv = q_norm_w.shape[0], kv_norm_w.shape[0]
    alpha = (2.0 * depth) ** 0.25
    ada = _ada(c.reshape(d), w_ada, b_ada)
    shift1, scale1, gate1, shift2, scale2, gate2 = [ada[:, k * d:(k + 1) * d] for k in range(6)]

    h = _ln_mod(x, scale1, shift1)
    n_big = 2 * d_inner + 2 * gn
    o_dt = n_big
    o_q = o_dt + nh
    o_kv = o_q + rq
    o_kr = o_kv + rkv
    o_g = o_kr + MLA_ROPE
    proj = _mm(h, w_in, col0=0, n=n_big, name="mm_in")
    pad = jnp.zeros((d, LANES - nh - MLA_ROPE), F32)
    w_small = jnp.concatenate([w_in[:, o_q:o_q + rq], w_in[:, o_kv:o_kv + rkv],
                               w_in[:, o_dt:o_dt + nh], pad, w_in[:, o_kr:o_kr + MLA_ROPE]], axis=1)
    small = _mm(h, w_small, out_dtype=F32, tn=w_small.shape[1], name="mm_small")
    glog = _mm(h, w_in[:, o_g:], name="mm_gate")
    yn = _ssd(proj, small, conv_w, conv_b, dt_bias, a_log, d_skip, ssd_norm_w)

    wq = jnp.pad(w_uq.reshape(rq, MLA_HEADS, MLA_QK), ((0, 0), (0, 0), (0, MLA_HEAD_PAD - MLA_QK)))
    wq = wq.reshape(rq, MLA_HEADS * MLA_HEAD_PAD).astype(BF16)
    wkv = w_ukv.reshape(rkv, MLA_HEADS, MLA_NOPE + MLA_V)
    wk = wkv[:, :, :MLA_NOPE].reshape(rkv, MLA_HEADS * MLA_NOPE).astype(BF16)
    wv = wkv[:, :, MLA_NOPE:].reshape(rkv, MLA_HEADS * MLA_V).astype(BF16)
    qf, kf, v = _mla_proj(small, q_norm_w, wq, kv_norm_w, wk, wv, *cs_tabs)
    o = _attn(qf, kf, v)

    mixin = _merge(yn, o, w_ssd_out, w_mla_out, glog, b_gate)
    mix = _mm(mixin, w_mix_out, out_dtype=F32, name="mm_mix")
    x1, h2 = _post_ln_mid(x, mix, gate1, ln1_g, ln1_b, scale2, shift2, alpha)

    q = _mm(h2, peer_w_q, name="mm_peer_q")
    p1, pth, p2 = _peer_thr(q, keys_1, keys_2)
    ft = _peer_main(h2, peer_u.astype(BF16), peer_v.T.astype(BF16), p1, pth, p2)
    return _post_ln_out(x1, ft, gate2, ln2_g, ln2_b, alpha)


def kernel(x, c, positions, w_ada, b_ada, w_in, b_gate, conv_w, conv_b, dt_bias, A_log, D_skip, ssd_norm_w, w_ssd_out, q_norm_w, w_uq, kv_norm_w, w_ukv, w_mla_out, w_mix_out, ln1_g, ln1_b, peer_w_q, peer_keys_1, peer_keys_2, peer_u, peer_v, ln2_g, ln2_b):
    b, s, d = x.shape
    depth = w_ada.shape[0]
    outs = []
    for bi in range(b):
        xb = x[bi]
        tabs = _rope_tables(positions[bi])
        for l in range(depth):
            xb = _layer(depth, xb, c[bi], tabs, w_ada[l], b_ada[l], w_in[l], b_gate[l], conv_w[l], conv_b[l], dt_bias[l],
                        A_log[l], D_skip[l], ssd_norm_w[l], w_ssd_out[l], q_norm_w[l], w_uq[l], kv_norm_w[l],
                        w_ukv[l], w_mla_out[l], w_mix_out[l], ln1_g[l], ln1_b[l], peer_w_q[l], peer_keys_1[l],
                        peer_keys_2[l], peer_u[l], peer_v[l], ln2_g[l], ln2_b[l])
        outs.append(xb)
    return jnp.stack(outs, axis=0)
```

```python
import functools

import numpy as np
import jax
import jax.numpy as jnp
from jax import lax
from jax.experimental import pallas as pl
from jax.experimental.pallas import tpu as pltpu

F32 = jnp.float32
BF16 = jnp.bfloat16
HI = lax.Precision.HIGHEST
NT_DIMS = (((1,), (1,)), ((), ()))

CHUNK = 64
EPS = 1e-5
SSD_HEAD_DIM = 64
SSD_GROUPS = 8
SSD_STATE = 128
SSD_CONV = 4
MLA_HEADS = 16
MLA_NOPE = 128
MLA_ROPE = 64
MLA_V = 128
MLA_QK = MLA_NOPE + MLA_ROPE
MLA_HEAD_PAD = 256
ROPE_THETA = 10000.0
N_BRANCH = 2
PEER_HEADS = 8
PEER_N_KEYS = 128
PEER_TOPK = 16
PEER_HALF = 128
LANES = 128
VMEM_LIMIT = 56 * 1024 * 1024


def _cp(*sem, vmem=VMEM_LIMIT):
    return pltpu.CompilerParams(dimension_semantics=sem, vmem_limit_bytes=vmem)


def _silu(v):
    return v * jax.nn.sigmoid(v)


def _normalize(v):
    mu = jnp.mean(v, axis=-1, keepdims=True)
    vc = v - mu
    var = jnp.mean(vc * vc, axis=-1, keepdims=True)
    return vc * lax.rsqrt(var + EPS)


def _ada_body(c_ref, w_ref, b_ref, o_ref, *, rows):
    d = c_ref.shape[0]
    tn = o_ref.shape[1]

    def step(k, acc):
        r0 = pl.multiple_of(k * rows, rows)
        cc = c_ref[pl.ds(r0, rows), :]
        return acc + w_ref[pl.ds(r0, rows), :] * _silu(cc)

    acc = lax.fori_loop(0, d // rows, step, jnp.zeros((rows, tn), F32))
    o_ref[...] = jnp.sum(acc, axis=0, keepdims=True) + b_ref[...]


def _ada(c, w_ada, b_ada, tn=512, rows=64):
    d, n = w_ada.shape
    return pl.pallas_call(
        functools.partial(_ada_body, rows=rows),
        grid=(n // tn,),
        in_specs=[pl.BlockSpec((d, 1), lambda j: (0, 0)),
                  pl.BlockSpec((d, tn), lambda j: (0, j)),
                  pl.BlockSpec((1, tn), lambda j: (0, j))],
        out_specs=pl.BlockSpec((1, tn), lambda j: (0, j)),
        out_shape=jax.ShapeDtypeStruct((1, n), F32),
        compiler_params=_cp("arbitrary"),
        name="ada",
    )(c.reshape(d, 1), w_ada, b_ada.reshape(1, n))


def _ln_mod_body(x_ref, sc_ref, sh_ref, o_ref):
    y = _normalize(x_ref[...])
    o_ref[...] = (y * (1.0 + sc_ref[...]) + sh_ref[...]).astype(o_ref.dtype)


def _ln_mod(x, scale, shift, tm=256):
    s, d = x.shape
    vec = pl.BlockSpec((1, d), lambda i: (0, 0))
    return pl.pallas_call(
        _ln_mod_body,
        grid=(s // tm,),
        in_specs=[pl.BlockSpec((tm, d), lambda i: (i, 0)), vec, vec],
        out_specs=pl.BlockSpec((tm, d), lambda i: (i, 0)),
        out_shape=jax.ShapeDtypeStruct((s, d), BF16),
        compiler_params=_cp("arbitrary"),
        name="ln_mod",
    )(x, scale, shift)


def _mm_body(a_ref, w_ref, o_ref, wbf_ref):
    @pl.when(pl.program_id(1) == 0)
    def _():
        wbf_ref[...] = w_ref[...].astype(BF16)

    o_ref[...] = jnp.dot(a_ref[...], wbf_ref[...], preferred_element_type=F32).astype(o_ref.dtype)


def _mm(a, w, *, col0=0, n=None, out_dtype=BF16, tm=1024, tn=1024, name="mm"):
    m, k = a.shape
    n = w.shape[1] - col0 if n is None else n
    tn = min(tn, n)
    tm = min(tm, m)
    assert n % tn == 0 and col0 % tn == 0 and m % tm == 0
    cb = col0 // tn
    return pl.pallas_call(
        _mm_body,
        grid=(n // tn, m // tm),
        in_specs=[pl.BlockSpec((tm, k), lambda j, i: (i, 0)),
                  pl.BlockSpec((k, tn), lambda j, i: (0, j + cb))],
        out_specs=pl.BlockSpec((tm, tn), lambda j, i: (i, j)),
        out_shape=jax.ShapeDtypeStruct((m, n), out_dtype),
        scratch_shapes=[pltpu.VMEM((k, tn), BF16)],
        compiler_params=_cp("arbitrary", "arbitrary"),
        name=name,
    )(a, w)


def _split3(v):
    hi = v.astype(BF16)
    r = v - hi.astype(F32)
    mid = r.astype(BF16)
    lo = (r - mid.astype(F32)).astype(BF16)
    return hi, mid, lo


def _expand(v, e):
    out = None
    for p in _split3(v):
        t = jnp.dot(p, e, preferred_element_type=F32)
        out = t if out is None else out + t
    return out


def _ssd_body(z_ref, xs_ref, bc_ref, dt_ref, cw_ref, cb_ref, dtb_ref, alog_ref, dexp_ref, nw_ref,
              e_ref, e128_ref, o_ref, xbuf, state, csx, fsx, wex, cst, dtt, xpost):
    L = CHUNK
    P = SSD_HEAD_DIM
    NS = SSD_STATE
    G = SSD_GROUPS
    d_inner = z_ref.shape[1]
    gw = d_inner // G
    hpg = gw // P
    nh = d_inner // P
    conv_dim = d_inner + 2 * G * NS
    halo = 8

    @pl.when(pl.program_id(0) == 0)
    def _():
        xbuf[0:halo, :] = jnp.zeros((halo, conv_dim), F32)
        state[...] = jnp.zeros(state.shape, F32)

    xbuf[halo:halo + L, 0:d_inner] = xs_ref[...].astype(F32)
    xbuf[halo:halo + L, d_inner:conv_dim] = bc_ref[...].astype(F32)
    slab = 512

    def conv_slab(j, _):
        c0 = pl.multiple_of(j * slab, slab)
        xe = xbuf[:, pl.ds(c0, slab)]
        w = cw_ref[:, pl.ds(c0, slab)]
        acc = cb_ref[:, pl.ds(c0, slab)] + xe[halo:halo + L] * w[SSD_CONV - 1:SSD_CONV]
        for sft in range(1, SSD_CONV):
            acc = acc + pltpu.roll(xe, sft, axis=0)[halo:halo + L] * w[SSD_CONV - 1 - sft:SSD_CONV - sft]
        xpost[:, pl.ds(c0, slab)] = _silu(acc)
        return 0

    lax.fori_loop(0, conv_dim // slab, conv_slab, 0)
    xbuf[0:halo, :] = xbuf[L:L + halo, :]

    ri = lax.broadcasted_iota(jnp.int32, (L, L), 0)
    ci = lax.broadcasted_iota(jnp.int32, (L, L), 1)
    tril = ri >= ci
    dt = jax.nn.softplus(dt_ref[...][:, 0:nh] + dtb_ref[...])
    ad = dt * (-jnp.exp(alog_ref[...]))
    cs = jnp.dot(tril.astype(F32), ad, precision=HI, preferred_element_type=F32)
    tot = cs[L - 1:L, :]
    wend = dt * jnp.exp(tot - cs)
    eye = (lax.broadcasted_iota(jnp.int32, (nh, nh), 0) == lax.broadcasted_iota(jnp.int32, (nh, nh), 1)).astype(F32)
    cst[...] = lax.dot_general(eye, cs, NT_DIMS, precision=HI, preferred_element_type=F32)
    dtt[...] = lax.dot_general(eye, dt, NT_DIMS, precision=HI, preferred_element_type=F32)
    csx[...] = _expand(cs, e128_ref[...])
    both = _expand(jnp.concatenate([cs, wend], axis=0), e_ref[...])
    fsx[...] = jnp.exp(both[0:L])
    wex[...] = both[L:2 * L]
    lane = lax.broadcasted_iota(jnp.int32, (L, LANES), 1)
    r128 = lax.broadcasted_iota(jnp.int32, (NS, NS), 0)
    c128 = lax.broadcasted_iota(jnp.int32, (NS, NS), 1)
    eye_n = (r128 == c128).astype(BF16)

    def group(g, _):
        o_g = pl.multiple_of(g * gw, gw)
        o_n = pl.multiple_of(g * NS, NS)
        bg = xpost[:, pl.ds(d_inner + o_n, NS)].astype(BF16)
        cg = xpost[:, pl.ds(d_inner + G * NS + o_n, NS)].astype(BF16)
        xg = xpost[:, pl.ds(o_g, gw)]
        xgb = xg.astype(BF16)
        cb = lax.dot_general(cg, bg, NT_DIMS, preferred_element_type=F32)
        st = state[g]
        fs_g = fsx[:, pl.ds(o_g, gw)]
        yoff = jnp.dot(cg, st.astype(BF16), preferred_element_type=F32) * fs_g
        parts = []
        for pair in range(gw // LANES):
            xpair = xgb[:, pair * LANES:(pair + 1) * LANES]
            zs = []
            for q in range(LANES // P):
                hh = pair * (LANES // P) + q
                h = g * hpg + hh
                row = csx[:, pl.ds(pl.multiple_of(h * LANES, LANES), LANES)][:, 0:L]
                col = cst[pl.ds(h, 1), :]
                dcol = dtt[pl.ds(h, 1), :]
                decay = jnp.exp(jnp.where(tril, row - col, -jnp.inf)) * dcol
                mh = (cb * decay).astype(BF16)
                zs.append(jnp.dot(mh, xpair, preferred_element_type=F32))
            acc = zs[-1]
            for q in range(LANES // P - 2, -1, -1):
                acc = jnp.where(lane < (q + 1) * P, zs[q], acc)
            parts.append(acc)
        ydiag = jnp.concatenate(parts, axis=1)
        xw = (xg * wex[:, pl.ds(o_g, gw)]).astype(BF16)
        bgt = lax.dot_general(eye_n, bg, NT_DIMS, preferred_element_type=F32).astype(BF16)
        new = jnp.dot(bgt, xw, preferred_element_type=F32)
        state[g] = st * fs_g[L - 1:L, :] + new
        y = ydiag + yoff + xg * dexp_ref[:, pl.ds(o_g, gw)]
        yg = y * _silu(z_ref[:, pl.ds(o_g, gw)].astype(F32))
        ms = jnp.mean(yg * yg, axis=-1, keepdims=True)
        o_ref[:, pl.ds(o_g, gw)] = (yg * lax.rsqrt(ms + EPS) * nw_ref[:, pl.ds(o_g, gw)]).astype(o_ref.dtype)
        return 0

    lax.fori_loop(0, G, group, 0)


def _ssd(proj, small, conv_w, conv_b, dt_bias, a_log, d_skip, norm_w):
    s = proj.shape[0]
    d_inner = norm_w.shape[0]
    nh = d_inner // SSD_HEAD_DIM
    gn = SSD_GROUPS * SSD_STATE
    conv_dim = d_inner + 2 * gn
    gw = d_inner // SSD_GROUPS
    L = CHUNK
    assert d_inner % gn == 0 and nh <= LANES
    dt_blk = (small.shape[1] - LANES) // LANES
    e =jnp.asarray(np.repeat(np.eye(nh, dtype=np.float32), SSD_HEAD_DIM, axis=1), BF16)
    e128 = jnp.asarray(np.repeat(np.eye(nh, dtype=np.float32), LANES, axis=1), BF16)
    dexp = jnp.repeat(d_skip, SSD_HEAD_DIM).reshape(1, d_inner)
    full = lambda shape: pl.BlockSpec(shape, lambda c: (0,) * len(shape))
    return pl.pallas_call(
        _ssd_body,
        grid=(s // L,),
        in_specs=[pl.BlockSpec((L, d_inner), lambda c: (c, 0)),
                  pl.BlockSpec((L, d_inner), lambda c: (c, 1)),
                  pl.BlockSpec((L, 2 * gn), lambda c: (c, d_inner // gn)),
                  pl.BlockSpec((L, LANES), lambda c: (c, dt_blk)),
                  full((SSD_CONV, conv_dim)), full((1, conv_dim)), full((1, nh)), full((1, nh)),
                  full((1, d_inner)), full((1, d_inner)), full((nh, d_inner)), full((nh, nh * LANES))],
        out_specs=pl.BlockSpec((L, d_inner), lambda c: (c, 0)),
        out_shape=jax.ShapeDtypeStruct((s, d_inner), BF16),
        scratch_shapes=[pltpu.VMEM((L + 8, conv_dim), F32),
                        pltpu.VMEM((SSD_GROUPS, SSD_STATE, gw), F32),
                        pltpu.VMEM((L, nh * LANES), F32),
                        pltpu.VMEM((L, d_inner), F32),
                        pltpu.VMEM((L, d_inner), F32),
                        pltpu.VMEM((nh, L), F32), pltpu.VMEM((nh, L), F32),
                        pltpu.VMEM((L, conv_dim), F32)],
        compiler_params=_cp("arbitrary"),
        name="ssd",
    )(proj, proj, proj, small, conv_w, conv_b.reshape(1, conv_dim), dt_bias.reshape(1, nh), a_log.reshape(1, nh),
      dexp, norm_w.reshape(1, d_inner), e, e128)


def _rope_body(pos_ref, inv_ref, c_ref, sa_ref, sb_ref):
    half = MLA_ROPE // 2
    ang = pos_ref[...].astype(F32) * inv_ref[...]
    lane = lax.broadcasted_iota(jnp.int32, ang.shape, 1)
    cos = jnp.cos(ang)
    sin = jnp.sin(ang)
    c_ref[...] = jnp.where(lane < MLA_ROPE, cos, 0.0)
    sa_ref[...] = jnp.where(lane < half, -sin, 0.0)
    sb_ref[...] = jnp.where((lane >= half) & (lane < MLA_ROPE), sin, 0.0)


def _rope_tables(positions, tm=1024):
    s = positions.shape[0]
    inv = ROPE_THETA ** (-jnp.arange(0, MLA_ROPE, 2, dtype=F32) / MLA_ROPE)
    inv = jnp.concatenate([inv, inv, jnp.zeros((LANES - MLA_ROPE,), F32)]).reshape(1, LANES)
    posb = jnp.broadcast_to(positions.reshape(s, 1), (s, LANES))
    blk = pl.BlockSpec((tm, LANES), lambda i: (i, 0))
    return pl.pallas_call(
        _rope_body,
        grid=(s // tm,),
        in_specs=[blk, pl.BlockSpec((1, LANES), lambda i: (0, 0))],
        out_specs=[blk, blk, blk],
        out_shape=[jax.ShapeDtypeStruct((s, LANES), F32)] * 3,
        compiler_params=_cp("arbitrary"),
        name="rope",
    )(posb, inv)


def _rope(v, c, sa, sb):
    half = MLA_ROPE // 2
    return v * c + pltpu.roll(v, LANES - half, axis=1) * sa + pltpu.roll(v, half, axis=1) * sb


def _rms(v, gain):
    return v * lax.rsqrt(jnp.mean(v * v, axis=-1, keepdims=True) + EPS) * gain


def _qproj_body(a_ref, g_ref, w_ref, c_ref, sa_ref, sb_ref, o_ref):
    scale = MLA_QK ** -0.5
    qn = _rms(a_ref[...], g_ref[...]).astype(BF16)
    c, sa, sb = c_ref[...], sa_ref[...], sb_ref[...]
    for h in range(MLA_HEADS):
        lo = h * MLA_HEAD_PAD
        q = jnp.dot(qn, w_ref[:, lo:lo + MLA_HEAD_PAD], preferred_element_type=F32)
        o_ref[:, lo:lo + LANES] = (q[:, 0:LANES] * scale).astype(o_ref.dtype)
        o_ref[:, lo + LANES:lo + 2 * LANES] = (_rope(q[:, LANES:2 * LANES], c, sa, sb) * scale).astype(o_ref.dtype)


def _kvproj_body(a_ref, kr_ref, g_ref, wk_ref, wv_ref, c_ref, sa_ref, sb_ref, k_ref, v_ref):
    kn = _rms(a_ref[...], g_ref[...]).astype(BF16)
    kr = pltpu.roll(kr_ref[...], LANES - MLA_ROPE, axis=1)
    kpe = _rope(kr, c_ref[...], sa_ref[...], sb_ref[...]).astype(k_ref.dtype)
    v_ref[...] = jnp.dot(kn, wv_ref[...], preferred_element_type=F32).astype(v_ref.dtype)
    for h in range(MLA_HEADS):
        lo = h * MLA_HEAD_PAD
        k_ref[:, lo:lo + LANES] = jnp.dot(kn, wk_ref[:, h * MLA_NOPE:(h + 1) * MLA_NOPE],
                                          preferred_element_type=F32).astype(k_ref.dtype)
        k_ref[:, lo + LANES:lo + 2 * LANES] = kpe


def _mla_proj(small, q_gain, wq, kv_gain, wk, wv, c, sa, sb, tm=256):
    s = small.shape[0]
    rank = q_gain.shape[0]
    assert kv_gain.shape[0] == rank and rank % LANES == 0
    tab = pl.BlockSpec((tm, LANES), lambda i: (i, 0))
    full = lambda a: pl.BlockSpec(a.shape, lambda i: (0, 0))
    qg, kg = q_gain.reshape(1, rank), kv_gain.reshape(1, rank)
    qf = pl.pallas_call(
        _qproj_body,
        grid=(s // tm,),
        in_specs=[pl.BlockSpec((tm, rank), lambda i: (i, 0)), full(qg), full(wq), tab, tab, tab],
        out_specs=pl.BlockSpec((tm, MLA_HEADS * MLA_HEAD_PAD), lambda i: (i, 0)),
        out_shape=jax.ShapeDtypeStruct((s, MLA_HEADS * MLA_HEAD_PAD), BF16),
        compiler_params=_cp("arbitrary"),
        name="qproj",
    )(small, qg, wq, c, sa, sb)
    kf, v = pl.pallas_call(
        _kvproj_body,
        grid=(s // tm,),
        in_specs=[pl.BlockSpec((tm, rank), lambda i: (i, 1)),
                  pl.BlockSpec((tm, LANES), lambda i: (i, 2 * rank // LANES)),
                  full(kg), full(wk), full(wv), tab, tab, tab],
        out_specs=[pl.BlockSpec((tm, MLA_HEADS * MLA_HEAD_PAD), lambda i: (i, 0)),
                   pl.BlockSpec((tm, MLA_HEADS * MLA_V), lambda i: (i, 0))],
        out_shape=[jax.ShapeDtypeStruct((s, MLA_HEADS * MLA_HEAD_PAD), BF16),
                   jax.ShapeDtypeStruct((s, MLA_HEADS * MLA_V), BF16)],
        compiler_params=_cp("arbitrary"),
        name="kvproj",
    )(small, small, kg, wk, wv, c, sa, sb)
    return qf, kf, v


def _attn_body(q_ref, k_ref, v_ref, o_ref, m_scr, l_scr, acc_scr, *, bq, bk):
    i = pl.program_id(1)
    m_scr[...] = jnp.full(m_scr.shape, -jnp.inf, F32)
    l_scr[...] = jnp.zeros(l_scr.shape, F32)
    acc_scr[...] = jnp.zeros(acc_scr.shape, F32)

    def block(r0, nr, j, masked):
        rows = slice(r0, r0 + nr)
        k0 = pl.multiple_of(j * bk, bk)
        s = lax.dot_general(q_ref[rows, :], k_ref[pl.ds(k0, bk), :], NT_DIMS, preferred_element_type=F32)
        if masked:
            qc = (i * bq + r0 + lax.broadcasted_iota(jnp.int32, (nr, bk), 0)) // CHUNK
            kc = (k0 + lax.broadcasted_iota(jnp.int32, (nr, bk), 1)) // CHUNK
            s = jnp.where(kc <= qc, s, -jnp.inf)
        m_prev = m_scr[rows, :]
        m_next = jnp.maximum(m_prev, jnp.max(s, axis=1, keepdims=True))
        p = jnp.exp(s - pltpu.repeat(m_next, bk // LANES, axis=1))
        alpha = jnp.exp(m_prev - m_next)
        l_scr[rows, :] = alpha * l_scr[rows, :] + jnp.sum(p, axis=1, keepdims=True)
        acc_scr[rows, :] = alpha * acc_scr[rows, :] + jnp.dot(p.astype(BF16), v_ref[pl.ds(k0, bk), :],
                                                            preferred_element_type=F32)
        m_scr[rows, :] = m_next

    nsub = bq // bk
    nfull = i * nsub

    def body(jp, _):
        for u in range(nsub):
            block(0, bq, jp * nsub + u, False)
        return 0

    lax.fori_loop(0, i, body, 0)
    for d in range(nsub):
        block(d * bk, bk, nfull + d, True)
        if d + 1 < nsub:
            block((d + 1) * bk, bq - (d + 1) * bk, nfull + d, False)
    o_ref[...] = (acc_scr[...] / l_scr[...]).astype(o_ref.dtype)


def _attn(qf, kf, v, bq=1024, bk=512):
    s = qf.shape[0]
    assert bq % bk == 0 and bk % CHUNK == 0 and MLA_V == LANES
    return pl.pallas_call(
        functools.partial(_attn_body, bq=bq, bk=bk),
        grid=(MLA_HEADS, s // bq),
        in_specs=[pl.BlockSpec((bq, MLA_HEAD_PAD), lambda h, i: (i, h)),
                  pl.BlockSpec((s, MLA_HEAD_PAD), lambda h, i: (0, h)),
                  pl.BlockSpec((s, MLA_V), lambda h, i: (0, h))],
        out_specs=pl.BlockSpec((bq, MLA_V), lambda h, i: (i, h)),
        out_shape=jax.ShapeDtypeStruct((s, MLA_HEADS * MLA_V), BF16),
        scratch_shapes=[pltpu.VMEM((bq, LANES), F32), pltpu.VMEM((bq, LANES), F32), pltpu.VMEM((bq, MLA_V), F32)],
        compiler_params=_cp("arbitrary", "arbitrary"),
        name="attn",
    )(qf, kf, v)


def _merge_body(ys_ref, ym_ref, ws_ref, wm_ref, g0_ref, g1_ref, b0_ref, b1_ref, o_ref, wsb, wmb):
    @pl.when(pl.program_id(1) == 0)
    def _():
        wsb[...] = ws_ref[...].astype(BF16)
        wmb[...] = wm_ref[...].astype(BF16)

    y_ssd = jnp.dot(ys_ref[...], wsb[...], preferred_element_type=F32)
    y_mla = jnp.dot(ym_ref[...], wmb[...], preferred_element_type=F32)
    g0 = jax.nn.sigmoid(g0_ref[...].astype(F32) + b0_ref[...])
    g1 = jax.nn.sigmoid(g1_ref[...].astype(F32) + b1_ref[...])
    o_ref[...] = (g0 * y_ssd + g1 * y_mla).astype(o_ref.dtype)


def _merge(yn, o, w_ssd_out, w_mla_out, glog, b_gate, tm=512, tn=512):
    s, d = yn.shape[0], w_ssd_out.shape[1]
    ks, km = w_ssd_out.shape[0], w_mla_out.shape[0]
    nb = d // tn
    bg = b_gate.reshape(1, N_BRANCH * d)
    return pl.pallas_call(
        _merge_body,
        grid=(nb, s // tm),
        in_specs=[pl.BlockSpec((tm, ks), lambda j, i: (i, 0)), pl.BlockSpec((tm, km), lambda j, i: (i, 0)),
                  pl.BlockSpec((ks, tn), lambda j, i: (0, j)), pl.BlockSpec((km, tn), lambda j, i: (0, j)),
                  pl.BlockSpec((tm, tn), lambda j, i: (i, j)), pl.BlockSpec((tm, tn), lambda j, i: (i, j + nb)),
                  pl.BlockSpec((1, tn), lambda j, i: (0, j)), pl.BlockSpec((1, tn), lambda j, i: (0, j + nb))],
        out_specs=pl.BlockSpec((tm, tn), lambda j, i: (i, j)),
        out_shape=jax.ShapeDtypeStruct((s, d), BF16),
        scratch_shapes=[pltpu.VMEM((ks, tn), BF16), pltpu.VMEM((km, tn), BF16)],
        compiler_params=_cp("arbitrary", "arbitrary"),
        name="merge",
    )(yn, o, w_ssd_out, w_mla_out, glog, glog, bg, bg)


def _post_ln_mid_body(x_ref, f_ref, gate_ref, g_ref, b_ref, sc_ref, sh_ref, x_out, h_out, *, alpha):
    y = _normalize(alpha * x_ref[...] + (1.0 + gate_ref[...]) * f_ref[...]) * g_ref[...] + b_ref[...]
    x_out[...] = y
    h_out[...] = (_normalize(y) * (1.0 + sc_ref[...]) + sh_ref[...]).astype(h_out.dtype)


def _post_ln_mid(x, f, gate, g, b, scale, shift, alpha, tm=256):
    s, d = x.shape
    vec = pl.BlockSpec((1, d), lambda i: (0, 0))
    row = pl.BlockSpec((tm, d), lambda i: (i, 0))
    return pl.pallas_call(
        functools.partial(_post_ln_mid_body, alpha=alpha),
        grid=(s // tm,),
        in_specs=[row, row, vec, vec, vec, vec, vec],
        out_specs=[row, row],
        out_shape=[jax.ShapeDtypeStruct((s, d), F32), jax.ShapeDtypeStruct((s, d), BF16)],
        compiler_params=_cp("arbitrary"),
        name="post_ln_mid",
    )(x, f, gate, g.reshape(1, d), b.reshape(1, d), scale, shift)


def _post_ln_out_body(x_ref, ft_ref, gate_ref, g_ref, b_ref, x_out, *, alpha):
    f = ft_ref[...].T
    x_out[...] = _normalize(alpha * x_ref[...] + (1.0 + gate_ref[...]) * f) * g_ref[...] + b_ref[...]


def _post_ln_out(x, ft, gate, g, b, alpha, tm=256):
    s, d = x.shape
    vec = pl.BlockSpec((1, d), lambda i: (0, 0))
    row = pl.BlockSpec((tm, d), lambda i: (i, 0))
    return pl.pallas_call(
        functools.partial(_post_ln_out_body, alpha=alpha),
        grid=(s // tm,),
        in_specs=[row, pl.BlockSpec((d, tm), lambda i: (0, i)), vec, vec, vec],
        out_specs=row,
        out_shape=jax.ShapeDtypeStruct((s, d), F32),
        compiler_params=_cp("arbitrary"),
        name="post_ln_out",
    )(x, ft, gate, g.reshape(1, d), b.reshape(1, d))


def _topk_desc(v, k):
    vals = []
    cur = v
    for _ in range(k):
        m = jnp.max(cur, axis=0, keepdims=True)
        vals.append(m)
        cur = jnp.where(cur == m, -jnp.inf, cur)
    return jnp.concatenate(vals, axis=0)


def _peer_thr_body(q_ref, k1_ref, k2_ref, p1_ref, pth_ref, p2_ref):
    K = PEER_TOPK
    q = q_ref[...]
    s1 = lax.dot_general(k1_ref[...].astype(BF16), q[:, 0:PEER_HALF], NT_DIMS, preferred_element_type=F32)
    s2 = lax.dot_general(k2_ref[...].astype(BF16), q[:, PEER_HALF:2 * PEER_HALF], NT_DIMS,
                         preferred_element_type=F32)
    v1 = _topk_desc(s1, K)
    v2 = _topk_desc(s2, K)
    cand = jnp.concatenate([v1[a:a + 1] + v2 for a in range(K)], axis=0)
    tau = _topk_desc(cand, K)[K - 1:K]
    m1, m2 = v1[0:1], v2[0:1]
    e1 = jnp.exp(v1 - m1)
    e2 = jnp.exp(v2 - m2)
    ecand = jnp.concatenate([e1[a:a + 1] * e2 for a in range(K)], axis=0)
    z = jnp.sum(jnp.where(cand >= tau, ecand, 0.0), axis=0, keepdims=True)
    inv_z = 1.0 / z
    thr2 = jnp.full(s1.shape, jnp.inf, F32)
    for b in range(K):
        thr2 = jnp.where(s1 + v2[b:b + 1] >= tau, v2[b:b + 1], thr2)
    p1_ref[0] = jnp.exp(s1 - m1)
    p2_ref[0] = jnp.exp(s2 - m2) * inv_z
    pth_ref[0] = jnp.exp(thr2 - m2) * inv_z


def _peer_thr(q, keys_1, keys_2, tm=256):
    s = q.shape[0]
    nk = keys_1.shape[0]
    blk = pl.BlockSpec((1, nk, tm), lambda i, h: (h, 0, i))
    kspec = pl.BlockSpec(keys_1.shape, lambda i, h: (0, 0))
    out = jax.ShapeDtypeStruct((PEER_HEADS, nk, s), F32)
    return pl.pallas_call(
        _peer_thr_body,
        grid=(s // tm, PEER_HEADS),
        in_specs=[pl.BlockSpec((tm, 2 * PEER_HALF), lambda i, h: (i, h)), kspec, kspec],
        out_specs=[blk, blk, blk],
        out_shape=[out, out, out],
        compiler_params=_cp("arbitrary", "arbitrary"),
        name="peer_thr",
    )(q, keys_1, keys_2)


def _gelu(v):
    return 0.5 * v * (1.0 + lax.erf(v * (2.0 ** -0.5)))


def _peer_main_body(h_ref, u_ref, vt_ref, p1_ref, pth_ref, p2_ref, o_ref, at_scr, g_scr):
    e = pl.program_id(1)
    te, tm = g_scr.shape
    nk = PEER_N_KEYS
    nblk = te // nk

    @pl.when(e == 0)
    def _():
        o_ref[...] = jnp.zeros(o_ref.shape, F32)

    nsplit = 2
    tw = tm // nsplit
    for sp in range(nsplit):
        tcols = slice(sp * tw, (sp + 1) * tw)
        at_scr[:, tcols] = lax.dot_general(u_ref[...], h_ref[tcols, :], NT_DIMS, preferred_element_type=F32)
    for sp in range(nsplit):
        tcols = slice(sp * tw, (sp + 1) * tw)
        for lc in range(sp * tw // LANES, (sp + 1) * tw // LANES):
            cols = slice(lc * LANES, (lc + 1) * LANES)
            for ii in range(nblk):
                rows = slice(ii * nk, (ii + 1) * nk)
                w = jnp.zeros((nk, LANES), F32)
                for h in range(PEER_HEADS):
                    r0 = pl.multiple_of(h * nk + e * nblk, nblk)
                    th = pth_ref[pl.ds(r0, nblk), cols][ii:ii + 1]
                    p1 = p1_ref[pl.ds(r0, nblk), cols][ii:ii + 1]
                    p2 = p2_ref[h, :, cols]
                    w = w + jnp.where(p2 >= th, p2, 0.0) * p1
                g_scr[rows, cols] = (_gelu(at_scr[rows, cols]) * w).astype(g_scr.dtype)
        o_ref[:, tcols] += jnp.dot(vt_ref[...], g_scr[:, tcols], preferred_element_type=F32)


def _peer_main(h2, u_bf, vt_bf, p1, pth, p2, tm=512, te=1024):
    s, d = h2.shape
    ne = u_bf.shape[0]
    nk = PEER_N_KEYS
    assert te // nk == 8
    pblk = pl.BlockSpec((PEER_HEADS, nk, tm), lambda i, e: (0, 0, i))
    rblk = pl.BlockSpec((PEER_HEADS * nk, tm), lambda i, e: (0, i))
    return pl.pallas_call(
        _peer_main_body,
        grid=(s // tm, ne // te),
        in_specs=[pl.BlockSpec((tm, d), lambda i, e: (i, 0)),
                  pl.BlockSpec((te, d), lambda i, e: (e, 0)),
                  pl.BlockSpec((d, te), lambda i, e: (0, e)),
                  rblk, rblk, pblk],
        out_specs=pl.BlockSpec((d, tm), lambda i, e: (0, i)),
        out_shape=jax.ShapeDtypeStruct((d, s), F32),
        scratch_shapes=[pltpu.VMEM((te, tm), F32), pltpu.VMEM((te, tm), BF16)],
        compiler_params=_cp("arbitrary", "arbitrary"),
        name="peer_main",
    )(h2, u_bf, vt_bf, p1.reshape(PEER_HEADS * nk, s), pth.reshape(PEER_HEADS * nk, s), p2)


def _layer(depth, x, c, cs_tabs, w_ada, b_ada, w_in, b_gate, conv_w, conv_b, dt_bias, a_log, d_skip, ssd_norm_w, w_ssd_out,
           q_norm_w, w_uq, kv_norm_w, w_ukv, w_mla_out, w_mix_out, ln1_g, ln1_b, peer_w_q, keys_1, keys_2,
           peer_u, peer_v, ln2_g, ln2_b):
    s, d = x.shape
    d_inner = ssd_norm_w.shape[0]
    gn = SSD_GROUPS * SSD_STATE
    nh = d_inner // SSD_HEAD_DIM
    rq, rkv = q_norm_w.shape[0], kv_norm_w.shape[0]
    alpha = (2.0 * depth) ** 0.25
    ada = _ada(c.reshape(d), w_ada, b_ada)
    shift1, scale1, gate1, shift2, scale2, gate2 = [ada[:, k * d:(k + 1) * d] for k in range(6)]

    h = _ln_mod(x, scale1, shift1)
    n_big = 2 * d_inner + 2 * gn
    o_dt = n_big
    o_q = o_dt + nh
    o_kv = o_q + rq
    o_kr = o_kv + rkv
    o_g = o_kr + MLA_ROPE
    proj = _mm(h, w_in, col0=0, n=n_big, name="mm_in")
    pad = jnp.zeros((d, LANES - nh - MLA_ROPE), F32)
    w_small = jnp.concatenate([w_in[:, o_q:o_q + rq], w_in[:, o_kv:o_kv + rkv],
                               w_in[:, o_dt:o_dt + nh], pad, w_in[:, o_kr:o_kr + MLA_ROPE]], axis=1)
    small = _mm(h, w_small, out_dtype=F32, tn=w_small.shape[1], name="mm_small")
    glog = _mm(h, w_in[:, o_g:], name="mm_gate")
    yn = _ssd(proj, small, conv_w, conv_b, dt_bias, a_log, d_skip, ssd_norm_w)

    wq = jnp.pad(w_uq.reshape(rq, MLA_HEADS, MLA_QK), ((0, 0), (0, 0), (0, MLA_HEAD_PAD - MLA_QK)))
    wq = wq.reshape(rq, MLA_HEADS * MLA_HEAD_PAD).astype(BF16)
    wkv = w_ukv.reshape(rkv, MLA_HEADS, MLA_NOPE + MLA_V)
    wk = wkv[:, :, :MLA_NOPE].reshape(rkv, MLA_HEADS * MLA_NOPE).astype(BF16)
    wv = wkv[:, :, MLA_NOPE:].reshape(rkv, MLA_HEADS * MLA_V).astype(BF16)
    qf, kf, v = _mla_proj(small, q_norm_w, wq, kv_norm_w, wk, wv, *cs_tabs)
    o = _attn(qf, kf, v)

    mixin = _merge(yn, o, w_ssd_out, w_mla_out, glog, b_gate)
    mix = _mm(mixin, w_mix_out, out_dtype=F32, name="mm_mix")
    x1, h2 = _post_ln_mid(x, mix, gate1, ln1_g, ln1_b, scale2, shift2, alpha)

    q = _mm(h2, peer_w_q, name="mm_peer_q")
    p1, pth, p2 = _peer_thr(q, keys_1, keys_2)
    ft = _peer_main(h2, peer_u.astype(BF16), peer_v.T.astype(BF16), p1, pth, p2)
    return _post_ln_out(x1, ft, gate2, ln2_g, ln2_b, alpha)


def kernel(x, c, positions, w_ada, b_ada, w_in, b_gate, conv_w, conv_b, dt_bias, A_log, D_skip, ssd_norm_w, w_ssd_out, q_norm_w, w_uq, kv_norm_w, w_ukv, w_mla_out, w_mix_out, ln1_g, ln1_b, peer_w_q, peer_keys_1, peer_keys_2, peer_u, peer_v, ln2_g, ln2_b):
    b, s, d = x.shape
    depth = w_ada.shape[0]
    outs = []
    for bi in range(b):
        xb = x[bi]
        tabs = _rope_tables(positions[bi])
        for l in range(depth):
            xb = _layer(depth, xb, c[bi], tabs, w_ada[l], b_ada[l], w_in[l], b_gate[l], conv_w[l], conv_b[l], dt_bias[l],
                        A_log[l], D_skip[l], ssd_norm_w[l], w_ssd_out[l], q_norm_w[l], w_uq[l], kv_norm_w[l],
                        w_ukv[l], w_mla_out[l], w_mix_out[l], ln1_g[l], ln1_b[l], peer_w_q[l], peer_keys_1[l],
                        peer_keys_2[l], peer_u[l], peer_v[l], ln2_g[l], ln2_b[l])
        outs.append(xb)
    return jnp.stack(outs, axis=0)
```

```python
import functools

import numpy as np
import jax
import jax.numpy as jnp
from jax import lax
from jax.experimental import pallas as pl
from jax.experimental.pallas import tpu as pltpu

F32 = jnp.float32
BF16 = jnp.bfloat16
NT_DIMS = (((1,), (1,)), ((), ()))

CHUNK = 64
EPS = 1e-5
SSD_HEAD_DIM = 64
SSD_GROUPS = 8
SSD_STATE = 128
SSD_CONV = 4
MLA_HEADS = 16
MLA_NOPE = 128
MLA_ROPE = 64
MLA_V = 128
MLA_QK = MLA_NOPE + MLA_ROPE
MLA_HEAD_PAD = 256
ROPE_THETA = 10000.0
N_BRANCH = 2
PEER_HEADS = 8
PEER_N_KEYS = 128
PEER_TOPK = 16
PEER_HALF = 128
LANES = 128
VMEM_LIMIT = 56 * 1024 * 1024


def _cp(*sem, vmem=VMEM_LIMIT):
    return pltpu.CompilerParams(dimension_semantics=sem, vmem_limit_bytes=vmem)


def _silu(v):
    return v * jax.nn.sigmoid(v)


def _normalize(v):
    mu = jnp.mean(v, axis=-1, keepdims=True)
    vc = v - mu
    var = jnp.mean(vc * vc, axis=-1, keepdims=True)
    return vc * lax.rsqrt(var + EPS)


def _ada_body(c_ref, w_ref, b_ref, o_ref, *, rows):
    d = c_ref.shape[0]
    tn = o_ref.shape[1]

    def step(k, acc):
        r0 = pl.multiple_of(k * rows, rows)
        cc = c_ref[pl.ds(r0, rows), :]
        return acc + w_ref[pl.ds(r0, rows), :] * _silu(cc)

    acc = lax.fori_loop(0, d // rows, step, jnp.zeros((rows, tn), F32))
    o_ref[...] = jnp.sum(acc, axis=0, keepdims=True) + b_ref[...]


def _ada(c, w_ada, b_ada, tn=512, rows=64):
    d, n = w_ada.shape
    return pl.pallas_call(
        functools.partial(_ada_body, rows=rows),
        grid=(n // tn,),
        in_specs=[pl.BlockSpec((d, 1), lambda j: (0, 0)),
                  pl.BlockSpec((d, tn), lambda j: (0, j)),
                  pl.BlockSpec((1, tn), lambda j: (0, j))],
        out_specs=pl.BlockSpec((1, tn), lambda j: (0, j)),
        out_shape=jax.ShapeDtypeStruct((1, n), F32),
        compiler_params=_cp("arbitrary"),
        name="ada",
    )(c.reshape(d, 1), w_ada, b_ada.reshape(1, n))


def _ln_mod_body(x_ref, sc_ref, sh_ref, o_ref):
    y = _normalize(x_ref[...])
    o_ref[...] = (y * (1.0 + sc_ref[...]) + sh_ref[...]).astype(o_ref.dtype)


def _ln_mod(x, scale, shift, tm=256):
    s, d = x.shape
    vec = pl.BlockSpec((1, d), lambda i: (0, 0))
    return pl.pallas_call(
        _ln_mod_body,
        grid=(s // tm,),
        in_specs=[pl.BlockSpec((tm, d), lambda i: (i, 0)), vec, vec],
        out_specs=pl.BlockSpec((tm, d), lambda i: (i, 0)),
        out_shape=jax.ShapeDtypeStruct((s, d), BF16),
        compiler_params=_cp("arbitrary"),
        name="ln_mod",
    )(x, scale, shift)


def _mm_body(a_ref, w_ref, o_ref, wbf_ref):
    @pl.when(pl.program_id(1) == 0)
    def _():
        wbf_ref[...] = w_ref[...].astype(BF16)

    o_ref[...] = jnp.dot(a_ref[...], wbf_ref[...], preferred_element_type=F32).astype(o_ref.dtype)


def _mm(a, w, *, col0=0, n=None, out_dtype=BF16, tm=1024, tn=1024, name="mm"):
    m, k = a.shape
    n = w.shape[1] - col0 if n is None else n
    tn = min(tn, n)
    tm = min(tm, m)
    assert n % tn == 0 and col0 % tn == 0 and m % tm == 0
    cb = col0 // tn
    return pl.pallas_call(
        _mm_body,
        grid=(n // tn, m // tm),
        in_specs=[pl.BlockSpec((tm, k), lambda j, i: (i, 0)),
                  pl.BlockSpec((k, tn), lambda j, i: (0, j + cb))],
        out_specs=pl.BlockSpec((tm, tn), lambda j, i: (i, j)),
        out_shape=jax.ShapeDtypeStruct((m, n), out_dtype),
        scratch_shapes=[pltpu.VMEM((k, tn), BF16)],
        compiler_params=_cp("arbitrary", "arbitrary"),
        name=name,
    )(a, w)


def _split3(v):
    hi = v.astype(BF16)
    r = v - hi.astype(F32)
    mid = r.astype(BF16)
    lo = (r - mid.astype(F32)).astype(BF16)
    return hi, mid, lo


def _dot01(a, b, dims=(((1,), (0,)), ((), ())), split=1):
    out = None
    for p in _split3(b if split == 1 else a):
        t = lax.dot_general(a if split == 1 else p, p if split == 1 else b, dims, preferred_element_type=F32)
        out = t if out is None else out + t
    return out


def _ssd_body(z_ref, xs_ref, bc_ref, dt_ref, cw_ref, cb_ref, dtb_ref, alog_ref, dexp_ref, nw_ref,
              e_ref, o_ref, xbuf, state, csx, wex, colt, xpost):
    L = CHUNK
    P = SSD_HEAD_DIM
    NS = SSD_STATE
    G = SSD_GROUPS
    T, d_inner = z_ref.shape
    NC = T // L
    gw = d_inner // G
    nh = d_inner // P
    conv_dim = d_inner + 2 * G * NS
    halo = 8
    iota = lambda shape, ax: lax.broadcasted_iota(jnp.int32, shape, ax)

    @pl.when(pl.program_id(0) == 0)
    def _():
        xbuf[0:halo, :] = jnp.zeros((halo, conv_dim), F32)
        state[...] = jnp.zeros(state.shape, F32)

    xbuf[halo:halo + T, 0:d_inner] = xs_ref[...].astype(F32)
    xbuf[halo:halo + T, d_inner:conv_dim] = bc_ref[...].astype(F32)
    slab = 512
    nslab = conv_dim // slab

    def conv_slab(j, _):
        cols = pl.ds(pl.multiple_of(j * slab, slab), slab)
        w = cw_ref[:, cols]
        for rc in range(NC):
            acc = cb_ref[:, cols]
            for sft in range(SSD_CONV):
                r0 = halo + rc * L - sft
                acc = acc + xbuf[r0:r0 + L, cols] * w[SSD_CONV - 1 - sft:SSD_CONV - sft]
            xpost[rc * L:(rc + 1) * L, cols] = _silu(acc)
        return 0

    lax.fori_loop(0, nslab, conv_slab, 0)
    xbuf[0:halo, :] = xbuf[T:T + halo, :]

    ri, ci = iota((T, T), 0), iota((T, T), 1)
    same = (ri // L) == (ci // L)
    dt = jax.nn.softplus(dt_ref[...][:, 0:nh] + dtb_ref[...])
    ad = dt * (-jnp.exp(alog_ref[...]))
    cs = _dot01(jnp.where(same & (ri >= ci), 1.0, 0.0).astype(BF16), ad)
    tot = _dot01(jnp.where(same, 1.0, 0.0).astype(BF16), ad)
    wend = dt * jnp.exp(tot - cs)
    both = _dot01(jnp.concatenate([cs, wend], axis=0), e_ref[...], split=0)
    csx[...] = both[0:T]
    wex[...] = both[T:2 * T]
    parity = iota((L, nh), 1) % 2
    pieces = []
    for src in (cs, dt):
        for c in range(NC):
            for q in range(2):
                pieces.append(jnp.where(parity == q, src[c * L:(c + 1) * L], 0.0))
    pair01 = jnp.where(iota((nh // 2, nh), 1) // 2 == iota((nh // 2, nh), 0), 1.0, 0.0).astype(BF16)
    colt[...] = _dot01(pair01, jnp.concatenate(pieces, axis=0), dims=NT_DIMS)
    mask2 = iota((L, 2 * L), 0) >= iota((L, 2 * L), 1) % L
    bdmask = iota((2 * L, LANES), 0) // L == iota((2 * L, LANES), 1) // P
    eye_n = jnp.where(iota((NS, NS), 0) == iota((NS, NS), 1), 1.0, 0.0).astype(BF16)

    def chunk(c, _):
        rows = pl.ds(pl.multiple_of(c * L, L), L)
        cs_cols = pl.ds(pl.multiple_of(c * 2 * L, 2 * L), 2 * L)
        dt_cols = pl.ds(pl.multiple_of((NC + c) * 2 * L, 2 * L), 2 * L)
        gl = [slice(g * gw, (g + 1) * gw) for g in range(G)]
        bg = [xpost[rows, d_inner + g * NS:d_inner + (g + 1) * NS].astype(BF16) for g in range(G)]
        cg = [xpost[rows, d_inner + (G + g) * NS:d_inner + (G + g + 1) * NS].astype(BF16) for g in range(G)]
        cb2 = [lax.dot_general(cg[g], jnp.concatenate([bg[g], bg[g]], axis=0), NT_DIMS,
                               preferred_element_type=F32) for g in range(G)]
        bgt = [lax.dot_general(eye_n, bg[g], NT_DIMS, preferred_element_type=F32).astype(BF16) for g in range(G)]
        yoff = [jnp.dot(cg[g], state[g].astype(BF16), preferred_element_type=F32) for g in range(G)]
        ydiag = []
        for g in range(G):
            parts = []
            for pp in range(gw // LANES):
                pr = g * (gw // LANES) + pp
                lanes = slice(g * gw + pp * LANES, g * gw + (pp + 1) * LANES)
                col = colt[pr:pr + 1, cs_cols]
                dcol = colt[pr:pr + 1, dt_cols]
                decay = jnp.exp(jnp.where(mask2, csx[rows, lanes] - col, -jnp.inf)) * dcol
                m2 = (cb2[g] * decay).astype(BF16)
                xp = xpost[rows, lanes]
                xbd = jnp.where(bdmask, jnp.concatenate([xp, xp], axis=0), 0.0).astype(BF16)
                parts.append(jnp.dot(m2, xbd, preferred_element_type=F32))
            ydiag.append(jnp.concatenate(parts, axis=1))
        for g in range(G):
            xg = xpost[rows, gl[g]]
            fs_g = jnp.exp(csx[rows, gl[g]])
            xw = (xg * wex[rows, gl[g]]).astype(BF16)
            state[g] = state[g] * fs_g[L - 1:L, :] + jnp.dot(bgt[g], xw, preferred_element_type=F32)
            y = ydiag[g] + yoff[g] * fs_g + xg * dexp_ref[:, gl[g]]
            yg = y * _silu(z_ref[rows, gl[g]].astype(F32))
            ms = jnp.mean(yg * yg, axis=-1, keepdims=True)
            o_ref[rows, gl[g]] = (yg * lax.rsqrt(ms + EPS) * nw_ref[:, gl[g]]).astype(o_ref.dtype)
        return 0

    lax.fori_loop(0, NC, chunk, 0)


def _ssd(proj, small, conv_w, conv_b, dt_bias, a_log, d_skip, norm_w, tm=256):
    s = proj.shape[0]
    d_inner = norm_w.shape[0]
    nh = d_inner // SSD_HEAD_DIM
    gn = SSD_GROUPS * SSD_STATE
    conv_dim = d_inner + 2 * gn
    gw = d_inner // SSD_GROUPS
    L = CHUNK
    assert d_inner % gn == 0 and nh <= LANES and 2 * L == LANES and 2 * SSD_HEAD_DIM == LANES and tm % L == 0
    dt_blk = (small.shape[1] - LANES) // LANES
    e = jnp.asarray(np.repeat(np.eye(nh, dtype=np.float32), SSD_HEAD_DIM, axis=1), BF16)
    dexp = jnp.repeat(d_skip, SSD_HEAD_DIM).reshape(1, d_inner)
    full = lambda shape: pl.BlockSpec(shape, lambda c: (0,) * len(shape))
    return pl.pallas_call(
        _ssd_body,
        grid=(s // tm,),
        in_specs=[pl.BlockSpec((tm, d_inner), lambda c: (c, 0)),
                  pl.BlockSpec((tm, d_inner), lambda c: (c, 1)),
                  pl.BlockSpec((tm, 2 * gn), lambda c: (c, d_inner // gn)),
                  pl.BlockSpec((tm, LANES), lambda c: (c, dt_blk)),
                  full((SSD_CONV, conv_dim)), full((1, conv_dim)), full((1, nh)), full((1, nh)),
                  full((1, d_inner)), full((1, d_inner)), full((nh, d_inner))],
        out_specs=pl.BlockSpec((tm, d_inner), lambda c: (c, 0)),
        out_shape=jax.ShapeDtypeStruct((s, d_inner), BF16),
        scratch_shapes=[pltpu.VMEM((tm + 8, conv_dim), F32),
                        pltpu.VMEM((SSD_GROUPS, SSD_STATE, gw), F32),
                        pltpu.VMEM((tm, d_inner), F32),
                        pltpu.VMEM((tm, d_inner), F32),
                        pltpu.VMEM((nh // 2, 4 * tm), F32),
                        pltpu.VMEM((tm, conv_dim), F32)],
        compiler_params=_cp("arbitrary"),
        name="ssd",
    )(proj, proj, proj, small, conv_w, conv_b.reshape(1, conv_dim), dt_bias.reshape(1, nh), a_log.reshape(1, nh),
      dexp, norm_w.reshape(1, d_inner), e)


def _rope_body(pos_ref, inv_ref, c_ref, sa_ref, sb_ref):
    half = MLA_ROPE // 2
    ang = pos_ref[...].astype(F32) * inv_ref[...]
    lane = lax.broadcasted_iota(jnp.int32, ang.shape, 1)
    cos = jnp.cos(ang)
    sin = jnp.sin(ang)
    c_ref[...] = jnp.where(lane < MLA_ROPE, cos, 0.0)
    sa_ref[...] = jnp.where(lane < half, -sin, 0.0)
    sb_ref[...] = jnp.where((lane >= half) & (lane < MLA_ROPE), sin, 0.0)


def _rope_tables(positions, tm=1024):
    s = positions.shape[0]
    inv = ROPE_THETA ** (-jnp.arange(0, MLA_ROPE, 2, dtype=F32) / MLA_ROPE)
    inv = jnp.concatenate([inv, inv, jnp.zeros((LANES - MLA_ROPE,), F32)]).reshape(1, LANES)
    posb = jnp.broadcast_to(positions.reshape(s, 1), (s, LANES))
    blk = pl.BlockSpec((tm, LANES), lambda i: (i, 0))
    return pl.pallas_call(
        _rope_body,
        grid=(s // tm,),
        in_specs=[blk, pl.BlockSpec((1, LANES), lambda i: (0, 0))],
        out_specs=[blk, blk, blk],
        out_shape=[jax.ShapeDtypeStruct((s, LANES), F32)] * 3,
        compiler_params=_cp("arbitrary"),
        name="rope",
    )(posb, inv)


def _rope(v, c, sa, sb):
    half = MLA_ROPE // 2
    return v * c + pltpu.roll(v, LANES - half, axis=1) * sa + pltpu.roll(v, half, axis=1) * sb


def _rms(v, gain):
    return v * lax.rsqrt(jnp.mean(v * v, axis=-1, keepdims=True) + EPS) * gain


def _qproj_body(a_ref, g_ref, w_ref, c_ref, sa_ref, sb_ref, o_ref):
    scale = MLA_QK ** -0.5
    qn = _rms(a_ref[...], g_ref[...]).astype(BF16)
    c, sa, sb = c_ref[...], sa_ref[...], sb_ref[...]
    for h in range(MLA_HEADS):
        lo = h * MLA_HEAD_PAD
        q = jnp.dot(qn, w_ref[:, lo:lo + MLA_HEAD_PAD], preferred_element_type=F32)
        o_ref[:, lo:lo + LANES] = (q[:, 0:LANES] * scale).astype(o_ref.dtype)
        o_ref[:, lo + LANES:lo + 2 * LANES] = (_rope(q[:, LANES:2 * LANES], c, sa, sb) * scale).astype(o_ref.dtype)


def _kvproj_body(a_ref, kr_ref, g_ref, wk_ref, wv_ref, c_ref, sa_ref, sb_ref, k_ref, v_ref):
    kn = _rms(a_ref[...], g_ref[...]).astype(BF16)
    kr = pltpu.roll(kr_ref[...], LANES - MLA_ROPE, axis=1)
    kpe = _rope(kr, c_ref[...], sa_ref[...], sb_ref[...]).astype(k_ref.dtype)
    v_ref[...] = lax.dot_general(wv_ref[...], kn, NT_DIMS, preferred_element_type=F32).astype(v_ref.dtype)
    for h in range(MLA_HEADS):
        lo = h * MLA_HEAD_PAD
        k_ref[:, lo:lo + LANES] = jnp.dot(kn, wk_ref[:, h * MLA_NOPE:(h + 1) * MLA_NOPE],
                                          preferred_element_type=F32).astype(k_ref.dtype)
        k_ref[:, lo + LANES:lo + 2 * LANES] = kpe


def _mla_proj(small, q_gain, wq, kv_gain, wk, wv, c, sa, sb, tm=256):
    s = small.shape[0]
    rank = q_gain.shape[0]
    assert kv_gain.shape[0] == rank and rank % LANES == 0
    tab = pl.BlockSpec((tm, LANES), lambda i: (i, 0))
    full = lambda a: pl.BlockSpec(a.shape, lambda i: (0, 0))
    qg, kg = q_gain.reshape(1, rank), kv_gain.reshape(1, rank)
    qf = pl.pallas_call(
        _qproj_body,
        grid=(s // tm,),
        in_specs=[pl.BlockSpec((tm, rank), lambda i: (i, 0)), full(qg), full(wq), tab, tab, tab],
        out_specs=pl.BlockSpec((tm, MLA_HEADS * MLA_HEAD_PAD), lambda i: (i, 0)),
        out_shape=jax.ShapeDtypeStruct((s, MLA_HEADS * MLA_HEAD_PAD), BF16),
        compiler_params=_cp("arbitrary"),
        name="qproj",
    )(small, qg, wq, c, sa, sb)
    kf, v = pl.pallas_call(
        _kvproj_body,
        grid=(s // tm,),
        in_specs=[pl.BlockSpec((tm, rank), lambda i: (i, 1)),
                  pl.BlockSpec((tm, LANES), lambda i: (i, 2 * rank // LANES)),
                  full(kg), full(wk), full(wv), tab, tab, tab],
        out_specs=[pl.BlockSpec((tm, MLA_HEADS * MLA_HEAD_PAD), lambda i: (i, 0)),
                   pl.BlockSpec((MLA_HEADS * MLA_V, tm), lambda i: (0, i))],
        out_shape=[jax.ShapeDtypeStruct((s, MLA_HEADS * MLA_HEAD_PAD), BF16),
                   jax.ShapeDtypeStruct((MLA_HEADS * MLA_V, s), BF16)],
        compiler_params=_cp("arbitrary"),
        name="kvproj",
    )(small, small, kg, wk, wv, c, sa, sb)
    return qf, kf, v


def _attn_body(q_ref, k_ref, vt_ref, o_ref, m_scr, l_scr, acc_scr, st0, st1, *, bq, bk):
    i = pl.program_id(1)
    m_scr[...] = jnp.full(m_scr.shape, -jnp.inf, F32)
    l_scr[...] = jnp.zeros(l_scr.shape, F32)
    acc_scr[...] = jnp.zeros(acc_scr.shape, F32)

    def scores(c0, nc, j, dst):
        k0 = pl.multiple_of(j * bk, bk)
        dst[:, c0:c0 + nc] = lax.dot_general(k_ref[pl.ds(k0, bk), :], q_ref[c0:c0 + nc, :], NT_DIMS,
                                             preferred_element_type=F32)

    def update(c0, nc, j, src, masked):
        cols = slice(c0, c0 + nc)
        k0 = pl.multiple_of(j * bk, bk)
        st = src[:, cols]
        if masked:
            kc = (k0 + lax.broadcasted_iota(jnp.int32, (bk, nc), 0)) // CHUNK
            qc = (i * bq + c0 + lax.broadcasted_iota(jnp.int32, (bk, nc), 1)) // CHUNK
            st = jnp.where(kc <= qc, st, -jnp.inf)
        m_prev = m_scr[:, cols]
        m_next = jnp.maximum(m_prev, jnp.max(st, axis=0, keepdims=True))
        p = jnp.exp(st - m_next[0:1])
        alpha = jnp.exp(m_prev - m_next)
        l_scr[:, cols] = alpha * l_scr[:, cols] + jnp.sum(p, axis=0, keepdims=True)
        acc_scr[:, cols] = alpha[0:1] * acc_scr[:, cols] + jnp.dot(vt_ref[:, pl.ds(k0, bk)], p.astype(BF16),
                                                                   preferred_element_type=F32)
        m_scr[:, cols] = m_next

    nfull = i * 2

    scores(0, bq, 0, st0)

    def body(p, _):
        scores(0, bq, 2 * p + 1, st1)
        update(0, bq, 2 * p, st0, False)
        scores(0, bq, 2 * p + 2, st0)
        update(0, bq, 2 * p + 1, st1, False)
        return 0

    lax.fori_loop(0, i, body, 0)
    scores(bk, bk, nfull + 1, st1)
    update(0, bq, nfull, st0, True)
    update(bk, bk, nfull + 1, st1, True)
    o_ref[...] = (acc_scr[...] / l_scr[0:1, :]).T.astype(o_ref.dtype)


def _attn(qf, kf, vt, bq=1024, bk=512):
    s = qf.shape[0]
    assert bq == 2 * bk and bk % CHUNK == 0 and MLA_V == LANES
    return pl.pallas_call(
        functools.partial(_attn_body, bq=bq, bk=bk),
        grid=(MLA_HEADS, s // bq),
        in_specs=[pl.BlockSpec((bq, MLA_HEAD_PAD), lambda h, i: (i, h)),
                  pl.BlockSpec((s, MLA_HEAD_PAD), lambda h, i: (0, h)),
                  pl.BlockSpec((MLA_V, s), lambda h, i: (h, 0))],
        out_specs=pl.BlockSpec((bq, MLA_V), lambda h, i: (i, h)),
        out_shape=jax.ShapeDtypeStruct((s, MLA_HEADS * MLA_V), BF16),
        scratch_shapes=[pltpu.VMEM((8, bq), F32), pltpu.VMEM((8, bq), F32), pltpu.VMEM((MLA_V, bq), F32),
                        pltpu.VMEM((bk, bq), F32), pltpu.VMEM((bk, bq), F32)],
        compiler_params=_cp("arbitrary", "arbitrary"),
        name="attn",
    )(qf, kf, vt)


def _merge_body(ys_ref, ym_ref, ws_ref, wm_ref, g0_ref, g1_ref, b0_ref, b1_ref, o_ref, wsb, wmb):
    @pl.when(pl.program_id(1) == 0)
    def _():
        wsb[...] = ws_ref[...].astype(BF16)
        wmb[...] = wm_ref[...].astype(BF16)

    y_ssd = jnp.dot(ys_ref[...], wsb[...], preferred_element_type=F32)
    y_mla = jnp.dot(ym_ref[...], wmb[...], preferred_element_type=F32)
    g0 = jax.nn.sigmoid(g0_ref[...].astype(F32) + b0_ref[...])
    g1 = jax.nn.sigmoid(g1_ref[...].astype(F32) + b1_ref[...])
    o_ref[...] = (g0 * y_ssd + g1 * y_mla).astype(o_ref.dtype)


def _merge(yn, o, w_ssd_out, w_mla_out, glog, b_gate, tm=512, tn=512):
    s, d = yn.shape[0], w_ssd_out.shape[1]
    ks, km = w_ssd_out.shape[0], w_mla_out.shape[0]
    nb = d // tn
    bg = b_gate.reshape(1, N_BRANCH * d)
    return pl.pallas_call(
        _merge_body,
        grid=(nb, s // tm),
        in_specs=[pl.BlockSpec((tm, ks), lambda j, i: (i, 0)), pl.BlockSpec((tm, km), lambda j, i: (i, 0)),
                  pl.BlockSpec((ks, tn), lambda j, i: (0, j)), pl.BlockSpec((km, tn), lambda j, i: (0, j)),
                  pl.BlockSpec((tm, tn), lambda j, i: (i, j)), pl.BlockSpec((tm, tn), lambda j, i: (i, j + nb)),
                  pl.BlockSpec((1, tn), lambda j, i: (0, j)), pl.BlockSpec((1, tn), lambda j, i: (0, j + nb))],
        out_specs=pl.BlockSpec((tm, tn), lambda j, i: (i, j)),
        out_shape=jax.ShapeDtypeStruct((s, d), BF16),
        scratch_shapes=[pltpu.VMEM((ks, tn), BF16), pltpu.VMEM((km, tn), BF16)],
        compiler_params=_cp("arbitrary", "arbitrary"),
        name="merge",
    )(yn, o, w_ssd_out, w_mla_out, glog, glog, bg, bg)


def _post_ln_mid_body(x_ref, f_ref, gate_ref, g_ref, b_ref, sc_ref, sh_ref, x_out, h_out, *, alpha):
    y = _normalize(alpha * x_ref[...] + (1.0 + gate_ref[...]) * f_ref[...]) * g_ref[...] + b_ref[...]
    x_out[...] = y
    h_out[...] = (_normalize(y) * (1.0 + sc_ref[...]) + sh_ref[...]).astype(h_out.dtype)


def _post_ln_mid(x, f, gate, g, b, scale, shift, alpha, tm=256):
    s, d = x.shape
    vec = pl.BlockSpec((1, d), lambda i: (0, 0))
    row = pl.BlockSpec((tm, d), lambda i: (i, 0))
    return pl.pallas_call(
        functools.partial(_post_ln_mid_body, alpha=alpha),
        grid=(s // tm,),
        in_specs=[row, row, vec, vec, vec, vec, vec],
        out_specs=[row, row],
        out_shape=[jax.ShapeDtypeStruct((s, d), F32), jax.ShapeDtypeStruct((s, d), BF16)],
        compiler_params=_cp("arbitrary"),
        name="post_ln_mid",
    )(x, f, gate, g.reshape(1, d), b.reshape(1, d), scale, shift)


def _post_ln_out_body(x_ref, ft_ref, gate_ref, g_ref, b_ref, x_out, *, alpha):
    f = ft_ref[...].T
    x_out[...] = _normalize(alpha * x_ref[...] + (1.0 + gate_ref[...]) * f) * g_ref[...] + b_ref[...]


def _post_ln_out(x, ft, gate, g, b, alpha, tm=256):
    s, d = x.shape
    vec = pl.BlockSpec((1, d), lambda i: (0, 0))
    row = pl.BlockSpec((tm, d), lambda i: (i, 0))
    return pl.pallas_call(
        functools.partial(_post_ln_out_body, alpha=alpha),
        grid=(s // tm,),
        in_specs=[row, pl.BlockSpec((d, tm), lambda i: (0, i)), vec, vec, vec],
        out_specs=row,
        out_shape=jax.ShapeDtypeStruct((s, d), F32),
        compiler_params=_cp("arbitrary"),
        name="post_ln_out",
    )(x, ft, gate, g.reshape(1, d), b.reshape(1, d))


def _topk_desc(vs, k):
    vals = [[] for _ in vs]
    curs = list(vs)
    for _ in range(k):
        ms = [jnp.max(c, axis=0, keepdims=True) for c in curs]
        for out, m in zip(vals, ms):
            out.append(m)
        curs = [jnp.where(c == m, -jnp.inf, c) for c, m in zip(curs, ms)]
    return [jnp.concatenate(v, axis=0) for v in vals]


def _pair_grid(v1, v2, op):
    K = PEER_TOPK
    h = K // 2
    blocks = [op(v1[0:1], v2)] + [op(v1[a:a + 1], v2[0:h]) for a in range(1, h)] + [op(v1[h:K], v2[0:1])]
    return jnp.concatenate(blocks, axis=0)


def _peer_thr_body(q_ref, k1_ref, k2_ref, p1_ref, pth_ref, p2_ref, *, heads):
    K = PEER_TOPK
    k1 = k1_ref[...].astype(BF16)
    k2 = k2_ref[...].astype(BF16)
    s1, s2 = [], []
    for h in range(heads):
        q1 = q_ref[:, (2 * h) * PEER_HALF:(2 * h + 1) * PEER_HALF]
        q2 = q_ref[:, (2 * h + 1) * PEER_HALF:(2 * h + 2) * PEER_HALF]
        s1.append(lax.dot_general(k1, q1, NT_DIMS, preferred_element_type=F32))
        s2.append(lax.dot_general(k2, q2, NT_DIMS, preferred_element_type=F32))
    tops = _topk_desc(s1 + s2, K)
    v1, v2 = tops[:heads], tops[heads:]
    cand = [_pair_grid(v1[h], v2[h], jnp.add) for h in range(heads)]
    tau = [t[K - 1:K] for t in _topk_desc(cand, K)]
    for h in range(heads):
        m1, m2 = v1[h][0:1], v2[h][0:1]
        ecand = _pair_grid(jnp.exp(v1[h] - m1), jnp.exp(v2[h] - m2), jnp.multiply)
        z = jnp.sum(jnp.where(cand[h] >= tau[h], ecand, 0.0), axis=0, keepdims=True)
        inv_z = 1.0 / z
        thr2 = jnp.full(s1[h].shape, jnp.inf, F32)
        for b in range(K):
            thr2 = jnp.where(s1[h] + v2[h][b:b + 1] >= tau[h], v2[h][b:b + 1], thr2)
        p1_ref[h] = jnp.exp(s1[h] - m1)
        p2_ref[h] = jnp.exp(s2[h] - m2) * inv_z
        pth_ref[h] = jnp.exp(thr2 - m2) * inv_z


def _peer_thr(q, keys_1, keys_2, tm=256, heads=2):
    s = q.shape[0]
    nk = keys_1.shape[0]
    assert PEER_HEADS % heads == 0 and PEER_TOPK == 16
    blk = pl.BlockSpec((heads, nk, tm), lambda i, h: (h, 0, i))
    kspec = pl.BlockSpec(keys_1.shape, lambda i, h: (0, 0))
    out = jax.ShapeDtypeStruct((PEER_HEADS, nk, s), F32)
    return pl.pallas_call(
        functools.partial(_peer_thr_body, heads=heads),
        grid=(s // tm, PEER_HEADS // heads),
        in_specs=[pl.BlockSpec((tm, heads * 2 * PEER_HALF), lambda i, h: (i, h)), kspec, kspec],
        out_specs=[blk, blk, blk],
        out_shape=[out, out, out],
        compiler_params=_cp("arbitrary", "arbitrary"),
        name="peer_thr",
    )(q, keys_1, keys_2)


def _gelu(v):
    return 0.5 * v * (1.0 + lax.erf(v * (2.0 ** -0.5)))


def _peer_main_body(h_ref, u_ref, vt_ref, p1_ref, pth_ref, p2_ref, o_ref, at_scr, g_scr):
    e = pl.program_id(1)
    te, tm = g_scr.shape
    nk = PEER_N_KEYS
    nblk = te // nk

    @pl.when(e == 0)
    def _():
        o_ref[...] = jnp.zeros(o_ref.shape, F32)

    nsplit = 2
    tw = tm // nsplit
    for sp in range(nsplit):
        tcols = slice(sp * tw, (sp + 1) * tw)
        at_scr[:, tcols] = lax.dot_general(u_ref[...], h_ref[tcols, :], NT_DIMS, preferred_element_type=F32)
    for sp in range(nsplit):
        tcols = slice(sp * tw, (sp + 1) * tw)
        for lc in range(sp * tw // LANES, (sp + 1) * tw // LANES):
            cols = slice(lc * LANES, (lc + 1) * LANES)
            for ii in range(nblk):
                rows = slice(ii * nk, (ii + 1) * nk)
                w = jnp.zeros((nk, LANES), F32)
                for h in range(PEER_HEADS):
                    r0 = pl.multiple_of(h * nk + e * nblk, nblk)
                    th = pth_ref[pl.ds(r0, nblk), cols][ii:ii + 1]
                    p1 = p1_ref[pl.ds(r0, nblk), cols][ii:ii + 1]
                    p2 = p2_ref[h, :, cols]
                    w = w + jnp.where(p2 >= th, p2, 0.0) * p1
                g_scr[rows, cols] = (_gelu(at_scr[rows, cols]) * w).astype(g_scr.dtype)
        o_ref[:, tcols] += jnp.dot(vt_ref[...], g_scr[:, tcols], preferred_element_type=F32)


def _peer_main(h2, u_bf, vt_bf, p1, pth, p2, tm=512, te=1024):
    s, d = h2.shape
    ne = u_bf.shape[0]
    nk = PEER_N_KEYS
    assert te // nk == 8
    pblk = pl.BlockSpec((PEER_HEADS, nk, tm), lambda i, e: (0, 0, i))
    rblk = pl.BlockSpec((PEER_HEADS * nk, tm), lambda i, e: (0, i))
    return pl.pallas_call(
        _peer_main_body,
        grid=(s // tm, ne // te),
        in_specs=[pl.BlockSpec((tm, d), lambda i, e: (i, 0)),
                  pl.BlockSpec((te, d), lambda i, e: (e, 0)),
                  pl.BlockSpec((d, te), lambda i, e: (0, e)),
                  rblk, rblk, pblk],
        out_specs=pl.BlockSpec((d, tm), lambda i, e: (0, i)),
        out_shape=jax.ShapeDtypeStruct((d, s), F32),
        scratch_shapes=[pltpu.VMEM((te, tm), F32), pltpu.VMEM((te, tm), BF16)],
        compiler_params=_cp("arbitrary", "arbitrary"),
        name="peer_main",
    )(h2, u_bf, vt_bf, p1.reshape(PEER_HEADS * nk, s), pth.reshape(PEER_HEADS * nk, s), p2)


def _layer(depth, x, c, cs_tabs, w_ada, b_ada, w_in, b_gate, conv_w, conv_b, dt_bias, a_log, d_skip, ssd_norm_w, w_ssd_out,
           q_norm_w, w_uq, kv_norm_w, w_ukv, w_mla_out, w_mix_out, ln1_g, ln1_b, peer_w_q, keys_1, keys_2,
           peer_u, peer_v, ln2_g, ln2_b):
    s, d = x.shape
    d_inner = ssd_norm_w.shape[0]
    gn = SSD_GROUPS * SSD_STATE
    nh = d_inner // SSD_HEAD_DIM
    rq, rkv = q_norm_w.shape[0], kv_norm_w.shape[0]
    alpha = (2.0 * depth) ** 0.25
    ada = _ada(c.reshape(d), w_ada, b_ada)
    shift1, scale1, gate1, shift2, scale2, gate2 = [ada[:, k * d:(k + 1) * d] for k in range(6)]

    h = _ln_mod(x, scale1, shift1)
    n_big = 2 * d_inner + 2 * gn
    o_dt = n_big
    o_q = o_dt + nh
    o_kv = o_q + rq
    o_kr = o_kv + rkv
    o_g = o_kr + MLA_ROPE
    proj = _mm(h, w_in, col0=0, n=n_big, name="mm_in")
    pad = jnp.zeros((d, LANES - nh - MLA_ROPE), F32)
    w_small = jnp.concatenate([w_in[:, o_q:o_q + rq], w_in[:, o_kv:o_kv + rkv],
                               w_in[:, o_dt:o_dt + nh], pad, w_in[:, o_kr:o_kr + MLA_ROPE]], axis=1)
    small = _mm(h, w_small, out_dtype=F32, tn=w_small.shape[1], name="mm_small")
    glog = _mm(h, w_in[:, o_g:], name="mm_gate")
    yn = _ssd(proj, small, conv_w, conv_b, dt_bias, a_log, d_skip, ssd_norm_w)

    wq = jnp.pad(w_uq.reshape(rq, MLA_HEADS, MLA_QK), ((0, 0), (0, 0), (0, MLA_HEAD_PAD - MLA_QK)))
    wq = wq.reshape(rq, MLA_HEADS * MLA_HEAD_PAD).astype(BF16)
    wkv = w_ukv.reshape(rkv, MLA_HEADS, MLA_NOPE + MLA_V)
    wk = wkv[:, :, :MLA_NOPE].reshape(rkv, MLA_HEADS * MLA_NOPE).astype(BF16)
    wvt = wkv[:, :, MLA_NOPE:].reshape(rkv, MLA_HEADS * MLA_V).T.astype(BF16)
    qf, kf, vt = _mla_proj(small, q_norm_w, wq, kv_norm_w, wk, wvt, *cs_tabs)
    o = _attn(qf, kf, vt)

    mixin = _merge(yn, o, w_ssd_out, w_mla_out, glog, b_gate)
    mix = _mm(mixin, w_mix_out, out_dtype=F32, name="mm_mix")
    x1, h2 = _post_ln_mid(x, mix, gate1, ln1_g, ln1_b, scale2, shift2, alpha)

    q = _mm(h2, peer_w_q, name="mm_peer_q")
    p1, pth, p2 = _peer_thr(q, keys_1, keys_2)
    ft = _peer_main(h2, peer_u.astype(BF16), peer_v.T.astype(BF16), p1, pth, p2)
    return _post_ln_out(x1, ft, gate2, ln2_g, ln2_b, alpha)


def kernel(x, c, positions, w_ada, b_ada, w_in, b_gate, conv_w, conv_b, dt_bias, A_log, D_skip, ssd_norm_w, w_ssd_out, q_norm_w, w_uq, kv_norm_w, w_ukv, w_mla_out, w_mix_out, ln1_g, ln1_b, peer_w_q, peer_keys_1, peer_keys_2, peer_u, peer_v, ln2_g, ln2_b):
    b, s, d = x.shape
    depth = w_ada.shape[0]
    outs = []
    for bi in range(b):
        xb = x[bi]
        tabs = _rope_tables(positions[bi])
        for l in range(depth):
            xb = _layer(depth, xb, c[bi], tabs, w_ada[l], b_ada[l], w_in[l], b_gate[l], conv_w[l], conv_b[l], dt_bias[l],
                        A_log[l], D_skip[l], ssd_norm_w[l], w_ssd_out[l], q_norm_w[l], w_uq[l], kv_norm_w[l],
                        w_ukv[l], w_mla_out[l], w_mix_out[l], ln1_g[l], ln1_b[l], peer_w_q[l], peer_keys_1[l],
                        peer_keys_2[l], peer_u[l], peer_v[l], ln2_g[l], ln2_b[l])
        outs.append(xb)
    return jnp.stack(outs, axis=0)
```
